```python
import jax
import jax.numpy as jnp
from jax import lax
import numpy as np

D_MODEL = 1024
BATCH = 8
SEQ = 2048
DEPTH = 4

N_MIXERS = 3
D_FF = 2816
NORM_EPS = 1e-6
N_SUBLAYER_NORMS = 6

SWA_HEADS = 16
SWA_KV_HEADS = 4
SWA_GROUP = SWA_HEADS // SWA_KV_HEADS
SWA_HEAD_DIM = 64
SWA_WINDOW = 128
SWA_BLOCK = 128

GLA_HEADS = 4
GLA_DK = D_MODEL // 2
GLA_DV = D_MODEL
GLA_DK_HEAD = GLA_DK // GLA_HEADS
GLA_DV_HEAD = GLA_DV // GLA_HEADS
GLA_GATE_RANK = 16
GLA_GATE_TEMP = 16.0
GLA_CHUNK = 64
GLA_NORM_EPS = 1e-5

RWKV_HEAD_SIZE = 64
RWKV_HEADS = D_MODEL // RWKV_HEAD_SIZE
RWKV_DECAY_LORA = 64
RWKV_AAA_LORA = 64
RWKV_GATE_LORA = 160
RWKV_LNX_EPS = 64e-5
RWKV_N_SHIFT = 6

N_SWA_LAYERS = (DEPTH + 2) // 3
N_GLA_LAYERS = (DEPTH + 1) // 3
N_RWKV_LAYERS = DEPTH // 3

kernel_name = 'hybrid_swa_gla_rwkv7_macaron'


def rmsnorm(x, g, eps=NORM_EPS):
    xf = x.astype(jnp.float32)
    y = xf * lax.rsqrt(jnp.mean(xf * xf, axis=-1, keepdims=True) + eps)
    return (y * g.astype(jnp.float32)).astype(x.dtype)


def swiglu(x, w_in, w_out):
    gate, up = jnp.split(x @ w_in, 2, axis=-1)
    return (jax.nn.silu(gate) * up) @ w_out


def swa_attention(x, w_qkv, b_qkv, sinks, w_o, b_o):
    bsz, seq, _ = x.shape
    nb = seq // SWA_BLOCK
    qkv = x @ w_qkv + b_qkv
    q, k, v = jnp.split(qkv, [SWA_HEADS * SWA_HEAD_DIM, (SWA_HEADS + SWA_KV_HEADS) * SWA_HEAD_DIM], axis=-1)
    q = q.reshape(bsz, nb, SWA_BLOCK, SWA_KV_HEADS, SWA_GROUP, SWA_HEAD_DIM)
    k = k.reshape(bsz, nb, SWA_BLOCK, SWA_KV_HEADS, SWA_HEAD_DIM)
    v = v.reshape(bsz, nb, SWA_BLOCK, SWA_KV_HEADS, SWA_HEAD_DIM)

    def with_prev(t):
        prev = jnp.concatenate([jnp.zeros_like(t[:, :1]), t[:, :-1]], axis=1)
        return jnp.concatenate([prev, t], axis=2)

    kb, vb = with_prev(k), with_prev(v)
    scores = jnp.einsum('bnqhgd,bnkhd->bnhgqk', q, kb).astype(jnp.float32) * (SWA_HEAD_DIM ** -0.5)
    q_rel = jnp.arange(SWA_BLOCK)[:, None] + SWA_BLOCK
    k_rel = jnp.arange(2 * SWA_BLOCK)[None, :]
    dist = q_rel - k_rel
    band = (dist >= 0) & (dist < SWA_WINDOW)
    has_prev = (jnp.arange(nb) > 0)[:, None, None] | (k_rel >= SWA_BLOCK)[None]
    valid = band[None] & has_prev
    scores = jnp.where(valid[None, :, None, None], scores, -jnp.inf)
    sink = sinks.astype(jnp.float32).reshape(SWA_KV_HEADS, SWA_GROUP)[None, None, :, :, None, None]
    sink = jnp.broadcast_to(sink, scores.shape[:-1] + (1,))
    probs = jax.nn.softmax(jnp.concatenate([scores, sink], axis=-1), axis=-1)[..., :-1]
    out = jnp.einsum('bnhgqk,bnkhd->bnqhgd', probs.astype(vb.dtype), vb)
    return out.reshape(bsz, seq, SWA_HEADS * SWA_HEAD_DIM) @ w_o + b_o


def gla_attention(x, w_in, w_gate2, b_gate, norm_g, w_o):
    bsz, seq, _ = x.shape
    nc = seq // GLA_CHUNK
    splits = [GLA_DK, 2 * GLA_DK, 2 * GLA_DK + GLA_DV, 2 * GLA_DK + 2 * GLA_DV]
    q, k, v, g_out, g_low = jnp.split(x @ w_in, splits, axis=-1)
    log_alpha = jax.nn.log_sigmoid((g_low @ w_gate2 + b_gate).astype(jnp.float32)) / GLA_GATE_TEMP

    def to_chunks(t, d):
        t = t.astype(jnp.float32).reshape(bsz, nc, GLA_CHUNK, GLA_HEADS, d)
        return t.transpose(1, 0, 3, 2, 4)

    qc = to_chunks(q, GLA_DK_HEAD) * (GLA_DK_HEAD ** -0.5)
    kc = to_chunks(k, GLA_DK_HEAD)
    vc = to_chunks(v, GLA_DV_HEAD)
    gc = to_chunks(log_alpha, GLA_DK_HEAD)
    causal = jnp.tril(jnp.ones((GLA_CHUNK, GLA_CHUNK), dtype=bool))[None, None, :, :, None]

    def chunk_step(state, inp):
        q_t, k_t, v_t, g_t = inp
        b = jnp.cumsum(g_t, axis=2)
        o_inter = jnp.einsum('bhik,bhkv->bhiv', q_t * jnp.exp(b), state)
        rel = jnp.where(causal, b[:, :, :, None, :] - b[:, :, None, :, :], -jnp.inf)
        scores = jnp.einsum('bhik,bhjk,bhijk->bhij', q_t, k_t, jnp.exp(rel))
        o_intra = jnp.einsum('bhij,bhjv->bhiv', scores, v_t)
        b_last = b[:, :, -1:, :]
        state = (state * jnp.exp(b_last[:, :, 0, :])[..., None]
                 + jnp.einsum('bhjk,bhjv->bhkv', k_t * jnp.exp(b_last - b), v_t))
        return state, o_inter + o_intra

    state0 = jnp.zeros((bsz, GLA_HEADS, GLA_DK_HEAD, GLA_DV_HEAD), jnp.float32)
    _, o = lax.scan(chunk_step, state0, (qc, kc, vc, gc))
    o = o.transpose(1, 0, 3, 2, 4).reshape(bsz, seq, GLA_HEADS, GLA_DV_HEAD)
    o = rmsnorm(o, norm_g, GLA_NORM_EPS).reshape(bsz, seq, GLA_DV)
    o = (o * jax.nn.silu(g_out.astype(jnp.float32))).astype(x.dtype)
    return o @ w_o


def rwkv7_time_mix(x, mu, w_rkv, w0, w1, w2, a0, a1, a2, g1, g2, k_k, k_a, r_k, lnx_g, lnx_b, w_o):
    bsz, seq, _ = x.shape
    H, N = RWKV_HEADS, RWKV_HEAD_SIZE
    f32 = jnp.float32
    xx = jnp.pad(x, ((0, 0), (1, 0), (0, 0)))[:, :-1] - x
    xr, xw, xk, xv, xa, xg = (x + xx * mu[i] for i in range(RWKV_N_SHIFT))
    r = xr @ w_rkv[0]
    k = xk @ w_rkv[1]
    v = xv @ w_rkv[2]
    w = -jax.nn.softplus(-(w0 + jnp.tanh(xw @ w1) @ w2).astype(f32)) - 0.5
    decay = jnp.exp(-jnp.exp(w))
    a = jax.nn.sigmoid((a0 + (xa @ a1) @ a2).astype(f32))
    g = jax.nn.sigmoid(xg @ g1) @ g2
    kk = (k * k_k).astype(f32).reshape(bsz, seq, H, N)
    kk = kk / jnp.maximum(jnp.sqrt(jnp.sum(kk * kk, axis=-1, keepdims=True)), 1e-12)
    k = k.astype(f32) * (1.0 + (a - 1.0) * k_a.astype(f32))

    def heads(t):
        return t.astype(f32).reshape(bsz, seq, H, N)

    r_h, k_h, v_h, a_h, w_h = heads(r), heads(k), heads(v), heads(a), heads(decay)
    tm = lambda t: t.transpose(1, 0, 2, 3)
    xs = (tm(r_h), tm(w_h), tm(k_h), tm(v_h), tm(-kk), tm(kk * a_h))

    def step(state, inp):
        r_t, w_t, k_t, v_t, a_t, b_t = inp
        sa = jnp.einsum('bhij,bhj->bhi', state, a_t)
        state = (state * w_t[:, :, None, :] + sa[..., None] * b_t[:, :, None, :]
                 + v_t[..., None] * k_t[:, :, None, :])
        return state, jnp.einsum('bhij,bhj->bhi', state, r_t)

    _, y = lax.scan(step, jnp.zeros((bsz, H, N, N), f32), xs)
    y = y.transpose(1, 0, 2, 3)
    mean = jnp.mean(y, axis=-1, keepdims=True)
    var = jnp.mean(jnp.square(y - mean), axis=-1, keepdims=True)
    y = ((y - mean) * lax.rsqrt(var + RWKV_LNX_EPS)).reshape(bsz, seq, D_MODEL)
    y = y * lnx_g.astype(f32) + lnx_b.astype(f32)
    bonus = jnp.sum(r_h * k_h * r_k.astype(f32), axis=-1, keepdims=True) * v_h
    y = y + bonus.reshape(bsz, seq, D_MODEL)
    return (y * g.astype(f32)).astype(x.dtype) @ w_o


def setup_inputs(seed: int = 0) -> dict:
    key = jax.random.key(seed)
    ks = iter(list(jax.random.split(key, 32)))

    def nrm(shape, scale):
        return jax.random.normal(next(ks), shape, jnp.float32) * scale

    nA, nB, nC = N_SWA_LAYERS, N_GLA_LAYERS, N_RWKV_LAYERS
    qkv_w = (SWA_HEADS + 2 * SWA_KV_HEADS) * SWA_HEAD_DIM
    gla_in_w = 2 * GLA_DK + 2 * GLA_DV + GLA_GATE_RANK
    D = D_MODEL
    return {
        'x': nrm((BATCH, SEQ, D), 1.0),
        'norm_g': 1.0 + nrm((DEPTH, N_SUBLAYER_NORMS, D), 0.02),
        'ffn_w_in': nrm((DEPTH, 2, D, 2 * D_FF), D ** -0.5),
        'ffn_w_out': nrm((DEPTH, 2, D_FF, D), D_FF ** -0.5),
        'swa_w_qkv': nrm((nA, D, qkv_w), D ** -0.5),
        'swa_b_qkv': nrm((nA, qkv_w), 0.02),
        'swa_sinks': nrm((nA, SWA_HEADS), 1.0),
        'swa_w_o': nrm((nA, SWA_HEADS * SWA_HEAD_DIM, D), (SWA_HEADS * SWA_HEAD_DIM) ** -0.5),
        'swa_b_o': nrm((nA, D), 0.02),
        'gla_w_in': nrm((nB, D, gla_in_w), D ** -0.5),
        'gla_w_gate2': nrm((nB, GLA_GATE_RANK, GLA_DK), GLA_GATE_RANK ** -0.5),
        'gla_b_gate': nrm((nB, GLA_DK), 0.1),
        'gla_norm_g': 1.0 + nrm((nB, GLA_DV_HEAD), 0.02),
        'gla_w_o': nrm((nB, GLA_DV, D), GLA_DV ** -0.5),
        'rwkv_mu': jax.random.uniform(next(ks), (nC, RWKV_N_SHIFT, D), jnp.float32),
        'rwkv_w_rkv': nrm((nC, 3, D, D), D ** -0.5),
        'rwkv_w0': jax.random.uniform(next(ks), (nC, D), jnp.float32, -6.0, -1.0),
        'rwkv_w1': nrm((nC, D, RWKV_DECAY_LORA), D ** -0.5),
        'rwkv_w2': nrm((nC, RWKV_DECAY_LORA, D), 0.1 * RWKV_DECAY_LORA ** -0.5),
        'rwkv_a0': nrm((nC, D), 0.1),
        'rwkv_a1': nrm((nC, D, RWKV_AAA_LORA), D ** -0.5),
        'rwkv_a2': nrm((nC, RWKV_AAA_LORA, D), 0.1 * RWKV_AAA_LORA ** -0.5),
        'rwkv_g1': nrm((nC, D, RWKV_GATE_LORA), D ** -0.5),
        'rwkv_g2': nrm((nC, RWKV_GATE_LORA, D), RWKV_GATE_LORA ** -0.5),
        'rwkv_k_k': 0.85 + nrm((nC, D), 0.02),
        'rwkv_k_a': 1.0 + nrm((nC, D), 0.02),
        'rwkv_r_k': nrm((nC, RWKV_HEADS, RWKV_HEAD_SIZE), 0.1),
        'rwkv_lnx_g': 1.0 + nrm((nC, D), 0.02),
        'rwkv_lnx_b': nrm((nC, D), 0.02),
        'rwkv_w_o': nrm((nC, D, D), D ** -0.5),
    }


def reference(x, norm_g, ffn_w_in, ffn_w_out,
              swa_w_qkv, swa_b_qkv, swa_sinks, swa_w_o, swa_b_o,
              gla_w_in, gla_w_gate2, gla_b_gate, gla_norm_g, gla_w_o,
              rwkv_mu, rwkv_w_rkv, rwkv_w0, rwkv_w1, rwkv_w2, rwkv_a0, rwkv_a1, rwkv_a2,
              rwkv_g1, rwkv_g2, rwkv_k_k, rwkv_k_a, rwkv_r_k, rwkv_lnx_g, rwkv_lnx_b, rwkv_w_o):
    h = x
    for layer in range(DEPTH):
        g = norm_g[layer]
        ff = swiglu(rmsnorm(h, g[0]), ffn_w_in[layer, 0], ffn_w_out[layer, 0])
        h = h + 0.5 * rmsnorm(ff, g[1])
        u = rmsnorm(h, g[2])
        kind = layer % N_MIXERS
        j = layer // N_MIXERS
        if kind == 0:
            m = swa_attention(u, swa_w_qkv[j], swa_b_qkv[j], swa_sinks[j], swa_w_o[j], swa_b_o[j])
        elif kind == 1:
            m = gla_attention(u, gla_w_in[j], gla_w_gate2[j], gla_b_gate[j], gla_norm_g[j], gla_w_o[j])
        else:
            m = rwkv7_time_mix(u, rwkv_mu[j], rwkv_w_rkv[j], rwkv_w0[j], rwkv_w1[j], rwkv_w2[j],
                               rwkv_a0[j], rwkv_a1[j], rwkv_a2[j], rwkv_g1[j], rwkv_g2[j],
                               rwkv_k_k[j], rwkv_k_a[j], rwkv_r_k[j], rwkv_lnx_g[j], rwkv_lnx_b[j],
                               rwkv_w_o[j])
        h = h + rmsnorm(m, g[3])
        ff = swiglu(rmsnorm(h, g[4]), ffn_w_in[layer, 1], ffn_w_out[layer, 1])
        h = h + 0.5 * rmsnorm(ff, g[5])
    return h
```

```python
import functools

import jax
import jax.numpy as jnp
from jax import lax
from jax.experimental import pallas as pl
from jax.experimental.pallas import tpu as pltpu

F32 = jnp.float32
BF16 = jnp.bfloat16

NORM_EPS = 1e-6
GLA_NORM_EPS = 1e-5
GLA_GATE_TEMP = 16.0
RWKV_LNX_EPS = 64e-5

D_FF_CHUNK = 256
TOKEN_BLOCK = 512
SWA_BLOCK = 128
SWA_HEAD_DIM = 64
SWA_HEADS = 16
SWA_KV_HEADS = 4
GLA_HEADS = 4
GLA_CHUNK = 64
GLA_SEQ_BLOCK = 512
RWKV_HEAD = 64
RWKV_CHUNK = 64
RWKV_SEQ_BLOCK = 512
LANES = 128
VMEM_LIMIT = 56 * 1024 * 1024


def _params(sem):
    return pltpu.CompilerParams(dimension_semantics=sem, vmem_limit_bytes=VMEM_LIMIT)


def _rmsnorm(x, g, eps=NORM_EPS):
    return x * lax.rsqrt(jnp.mean(x * x, axis=-1, keepdims=True) + eps) * g


def _sigmoid(x):
    return 1.0 / (1.0 + jnp.exp(-x))


def _softplus(x):
    return jnp.maximum(x, 0.0) + jnp.log(1.0 + jnp.exp(-jnp.abs(x)))


def _dot(a, b):
    return jnp.dot(a, b, preferred_element_type=F32)


def _dot_nt(a, b):
    return lax.dot_general(a, b, (((1,), (1,)), ((), ())), preferred_element_type=F32)


def _dot_tn(a, b):
    return lax.dot_general(a, b, (((0,), (0,)), ((), ())), preferred_element_type=F32)


def _full(shape):
    n = len(shape)
    return pl.BlockSpec(shape, lambda *_: (0,) * n)


def _ffn_body(h_ref, g_ref, win_ref, wout_ref, o_ref, xn_ref, acc_ref, *, g_row, n_chunks):
    x = h_ref[...]
    xn_ref[...] = _rmsnorm(x, g_ref[g_row:g_row + 1, :]).astype(BF16)
    acc_ref[...] = jnp.zeros_like(acc_ref)

    def chunk(j, carry):
        xn = xn_ref[...]
        gate = _dot(xn, win_ref[0, j])
        up = _dot(xn, win_ref[1, j])
        act = (gate * _sigmoid(gate) * up).astype(BF16)
        acc_ref[...] += _dot(act, wout_ref[j])
        return carry

    lax.fori_loop(0, n_chunks, chunk, 0)
    o_ref[...] = x + 0.5 * _rmsnorm(acc_ref[...], g_ref[g_row + 1:g_row + 2, :])


def _ffn(h, g6, w_in, w_out, g_row):
    t, d = h.shape
    d_ff = w_out.shape[0]
    n_chunks = d_ff // D_FF_CHUNK
    win = w_in.astype(BF16).reshape(d, 2, n_chunks, D_FF_CHUNK).transpose(1, 2, 0, 3)
    wout = w_out.astype(BF16).reshape(n_chunks, D_FF_CHUNK, d)
    tm = TOKEN_BLOCK
    return pl.pallas_call(
        functools.partial(_ffn_body, g_row=g_row, n_chunks=n_chunks),
        grid=(t // tm,),
        in_specs=[
            pl.BlockSpec((tm, d), lambda i: (i, 0)),
            _full(g6.shape),
            _full(win.shape),
            _full(wout.shape),
        ],
        out_specs=pl.BlockSpec((tm, d), lambda i: (i, 0)),
        out_shape=jax.ShapeDtypeStruct((t, d), F32),
        scratch_shapes=[pltpu.VMEM((tm, d), BF16), pltpu.VMEM((tm, d), F32)],
        compiler_params=_params(("parallel",)),
        name="ffn",
    )(h, g6, win, wout)


def _norm_proj_body(h_ref, g_ref, w_ref, b_ref, o_ref, *, g_row, n_chunk):
    xn = _rmsnorm(h_ref[...], g_ref[g_row:g_row + 1, :]).astype(BF16)
    n = w_ref.shape[1]
    for c in range(0, n, n_chunk):
        y = _dot(xn, w_ref[:, c:c + n_chunk]) + b_ref[:, c:c + n_chunk]
        o_ref[:, c:c + n_chunk] = y.astype(o_ref.dtype)


def _norm_proj(h, g6, g_row, w, b, out_dtype):
    t, d = h.shape
    n = w.shape[1]
    tm = TOKEN_BLOCK
    return pl.pallas_call(
        functools.partial(_norm_proj_body, g_row=g_row, n_chunk=512),
        grid=(t // tm,),
        in_specs=[
            pl.BlockSpec((tm, d), lambda i: (i, 0)),
            _full(g6.shape),
            _full(w.shape),
            _full(b.shape),
        ],
        out_specs=pl.BlockSpec((tm, n), lambda i: (i, 0)),
        out_shape=jax.ShapeDtypeStruct((t, n), out_dtype),
        compiler_params=_params(("parallel",)),
        name="norm_proj",
    )(h, g6, w, b)


def _proj_res_body(a_ref, w_ref, b_ref, h_ref, g_ref, o_ref, *, g_row):
    y = _dot(a_ref[...], w_ref[...]) + b_ref[...]
    o_ref[...] = h_ref[...] + _rmsnorm(y, g_ref[g_row:g_row + 1, :])


def _proj_res(a, w, b, h, g6, g_row):
    t, d = h.shape
    k = a.shape[1]
    tm = TOKEN_BLOCK
    return pl.pallas_call(
        functools.partial(_proj_res_body, g_row=g_row),
        grid=(t // tm,),
        in_specs=[
            pl.BlockSpec((tm, k), lambda i: (i, 0)),
            _full(w.shape),
            _full(b.shape),
            pl.BlockSpec((tm, d), lambda i: (i, 0)),
            _full(g6.shape),
        ],
        out_specs=pl.BlockSpec((tm, d), lambda i: (i, 0)),
        out_shape=jax.ShapeDtypeStruct((t, d), F32),
        compiler_params=_params(("parallel",)),
        name="proj_res",
    )(a, w, b, h, g6)


def _swa_body(sink_ref, q_ref, kp_ref, kc_ref, vp_ref, vc_ref, o_ref):
    n = pl.program_id(1)
    blk = SWA_BLOCK
    kcat = jnp.concatenate([kp_ref[...], kc_ref[...]], axis=0)
    vcat = jnp.concatenate([vp_ref[...], vc_ref[...]], axis=0)
    q_idx = lax.broadcasted_iota(jnp.int32, (blk, 2 * blk), 0) + blk
    k_idx = lax.broadcasted_iota(jnp.int32, (blk, 2 * blk), 1)
    dist = q_idx - k_idx
    valid = (dist >= 0) & (dist < blk) & ((k_idx >= blk) | (n > 0))
    lane_head = lax.broadcasted_iota(jnp.int32, (blk, SWA_KV_HEADS * SWA_HEAD_DIM), 1) // SWA_HEAD_DIM
    scale = SWA_HEAD_DIM ** -0.5
    groups = SWA_HEADS // SWA_KV_HEADS
    width = SWA_KV_HEADS * SWA_HEAD_DIM
    for s in range(groups):
        qs = q_ref[:, s * width:(s + 1) * width].astype(F32)
        out = jnp.zeros((blk, width), F32)
        for hh in range(SWA_KV_HEADS):
            sel = lane_head == hh
            qm = jnp.where(sel, qs, 0.0).astype(BF16)
            sc = _dot_nt(qm, kcat) * scale
            sc = jnp.where(valid, sc, -jnp.inf)
            sink = sink_ref[hh * groups + s]
            m = jnp.maximum(jnp.max(sc, axis=-1, keepdims=True), sink)
            p = jnp.exp(sc - m)
            denom = jnp.sum(p, axis=-1, keepdims=True) + jnp.exp(sink - m)
            p = p / denom
            o = _dot(p.astype(BF16), vcat)
            out = jnp.where(sel, o, out)
        o_ref[:, s * width:(s + 1) * width] = out.astype(o_ref.dtype)


def _swa_core(qkv, sinks, bsz, seq):
    nb = seq // SWA_BLOCK
    blk = SWA_BLOCK
    dq = SWA_HEADS * SWA_HEAD_DIM
    dkv = SWA_KV_HEADS * SWA_HEAD_DIM
    kcol = dq // dkv
    cur = lambda b, n: b * nb + n
    prev = lambda b, n: b * nb + jnp.maximum(n - 1, 0)
    return pl.pallas_call(
        _swa_body,
        grid=(bsz, nb),
        in_specs=[
            pl.BlockSpec(memory_space=pltpu.SMEM),
            pl.BlockSpec((blk, dq), lambda b, n: (cur(b, n), 0)),
            pl.BlockSpec((blk, dkv), lambda b, n: (prev(b, n), kcol)),
            pl.BlockSpec((blk, dkv), lambda b, n: (cur(b, n), kcol)),
            pl.BlockSpec((blk, dkv), lambda b, n: (prev(b, n), kcol + 1)),
            pl.BlockSpec((blk, dkv), lambda b, n: (cur(b, n), kcol + 1)),
        ],
        out_specs=pl.BlockSpec((blk, dq), lambda b, n: (cur(b, n), 0)),
        out_shape=jax.ShapeDtypeStruct((bsz * seq, dq), BF16),
        compiler_params=_params(("parallel", "parallel")),
        name="swa_core",
    )(sinks, qkv, qkv, qkv, qkv, qkv)


def _swa_layer(h, g6, w_qkv, b_qkv, sinks, w_o, b_o, bsz, seq):
    d = h.shape[1]
    hd, nh, nkv = SWA_HEAD_DIM, SWA_HEADS, SWA_KV_HEADS
    grp = nh // nkv
    dq = nh * hd
    wq = w_qkv[:, :dq].reshape(d, nkv, grp, hd).transpose(0, 2, 1, 3).reshape(d, dq)
    bq = b_qkv[:dq].reshape(nkv, grp, hd).transpose(1, 0, 2).reshape(dq)
    w = jnp.concatenate([wq, w_qkv[:, dq:]], axis=1).astype(BF16)
    b = jnp.concatenate([bq, b_qkv[dq:]])[None, :]
    wo = w_o.reshape(nkv, grp, hd, d).transpose(1, 0, 2, 3).reshape(dq, d).astype(BF16)
    qkv = _norm_proj(h, g6, 2, w, b, BF16)
    att = _swa_core(qkv, sinks, bsz, seq)
    return _proj_res(att, wo, b_o[None, :], h, g6, 3)


def _split3(x):
    hi = x.astype(BF16)
    r1 = x - hi.astype(F32)
    mid = r1.astype(BF16)
    lo = (r1 - mid.astype(F32)).astype(BF16)
    return hi, mid, lo


def _gla_gate_body(h_ref, g_ref, wl_ref, w2_ref, bg_ref, o_ref):
    tm = h_ref.shape[0]
    xn = _rmsnorm(h_ref[...], g_ref[2:3, :]).astype(BF16)
    g_low = _dot(xn, wl_ref[...]).astype(BF16)
    z = _dot(g_low, w2_ref[...]) + bg_ref[...]
    la = -_softplus(-z) * (1.0 / GLA_GATE_TEMP)
    row = lax.broadcasted_iota(jnp.int32, (tm, tm), 0)
    col = lax.broadcasted_iota(jnp.int32, (tm, tm), 1)
    tri = jnp.where((row >= col) & (row // GLA_CHUNK == col // GLA_CHUNK), 1.0, 0.0).astype(BF16)
    hi, mid, lo = _split3(la)
    o_ref[...] = _dot(tri, hi) + _dot(tri, mid) + _dot(tri, lo)


def _gla_gate(h, g6, w_low, w_gate2, b_gate):
    t, d = h.shape
    dk = w_gate2.shape[1]
    tm = TOKEN_BLOCK
    return pl.pallas_call(
        _gla_gate_body,
        grid=(t // tm,),
        in_specs=[
            pl.BlockSpec((tm, d), lambda i: (i, 0)),
            _full(g6.shape),
            _full(w_low.shape),
            _full(w_gate2.shape),
            _full(b_gate.shape),
        ],
        out_specs=pl.BlockSpec((tm, dk), lambda i: (i, 0)),
        out_shape=jax.ShapeDtypeStruct((t, dk), F32),
        compiler_params=_params(("parallel",)),
        name="gla_gate",
    )(h, g6, w_low, w_gate2, b_gate)


def _block_first(x, s):
    c, l = x.shape
    xb = x.reshape(c // s, s, l)
    return jnp.broadcast_to(xb[:, 0:1, :], (c // s, s, l)).reshape(c, l)


def _gla_body(q_ref, k_ref, v_ref, go_ref, b_ref, ng_ref, o_ref, st_ref, *, n_chunks):
    c = GLA_CHUNK
    dk = q_ref.shape[1]

    @pl.when(pl.program_id(2) == 0)
    def _():
        st_ref[...] = jnp.zeros_like(st_ref)

    row = lax.broadcasted_iota(jnp.int32, (c, 1), 0)
    ri = lax.broadcasted_iota(jnp.int32, (c, c), 0)
    ci = lax.broadcasted_iota(jnp.int32, (c, c), 1)

    def chunk(ic, carry):
        r0 = pl.multiple_of(ic * c, c)
        q = q_ref[pl.ds(r0, c), :].astype(F32) * (dk ** -0.5)
        k = k_ref[pl.ds(r0, c), :].astype(F32)
        v = v_ref[pl.ds(r0, c), :]
        b = b_ref[pl.ds(r0, c), :]
        st = st_ref[...]
        o = _dot_nt((q * jnp.exp(b)).astype(BF16), st.astype(BF16))
        a = jnp.zeros((c, c), F32)
        for s in (8, 16, 32):
            first = _block_first(b, s)
            nxt = jnp.concatenate([first[s:], first[:s]], axis=0)
            odd = (row // s) % 2 == 1
            qh = jnp.where(odd, q * jnp.exp(jnp.where(odd, b - first, 0.0)), 0.0)
            kh = jnp.where(odd, 0.0, k * jnp.exp(jnp.where(odd, 0.0, nxt - b)))
            al = _dot_nt(qh.astype(BF16), kh.astype(BF16))
            a = a + jnp.where(ri // (2 * s) == ci // (2 * s), al, 0.0)
        k3 = k.reshape(c // 8, 8, dk)
        b3 = b.reshape(c // 8, 8, dk)
        for jj in range(8):
            kj = jnp.broadcast_to(k3[:, jj:jj + 1, :], k3.shape).reshape(c, dk)
            bj = jnp.broadcast_to(b3[:, jj:jj + 1, :], b3.shape).reshape(c, dk)
            keep = row % 8 >= jj
            tt = q * kj * jnp.exp(jnp.where(keep, b - bj, 0.0))
            colv = jnp.sum(tt, axis=-1, keepdims=True)
            hit = (ci == 8 * (ri // 8) + jj) & (ri % 8 >= jj)
            a = a + jnp.where(hit, colv, 0.0)
        o = o + _dot(a.astype(BF16), v)
        o = _rmsnorm(o, ng_ref[...], GLA_NORM_EPS)
        go = go_ref[pl.ds(r0, c), :].astype(F32)
        o_ref[pl.ds(r0, c), :] = (o * (go * _sigmoid(go))).astype(o_ref.dtype)
        b_last = b[c - 1:c, :]
        kd = (k * jnp.exp(b_last - b)).astype(BF16)
        st_ref[...] = st * jnp.exp(b_last) + _dot_tn(v, kd)
        return carry

    lax.fori_loop(0, n_chunks, chunk, 0)


def _gla_core(y, bcum, norm_g, bsz, seq):
    nh = GLA_HEADS
    dk = bcum.shape[1] // nh
    dv = (y.shape[1] - 2 * nh * dk) // 2 // nh
    ts = GLA_SEQ_BLOCK
    ns = seq // ts
    rowblk = lambda b, h, s: b * ns + s
    kq = nh * dk // dk
    return pl.pallas_call(
        functools.partial(_gla_body, n_chunks=ts // GLA_CHUNK),
        grid=(bsz, nh, ns),
        in_specs=[
            pl.BlockSpec((ts, dk), lambda b, h, s: (rowblk(b, h, s), h)),
            pl.BlockSpec((ts, dk), lambda b, h, s: (rowblk(b, h, s), kq + h)),
            pl.BlockSpec((ts, dv), lambda b, h, s: (rowblk(b, h, s), (2 * nh * dk) // dv + h)),
            pl.BlockSpec((ts, dv), lambda b, h, s: (rowblk(b, h, s), (2 * nh * dk) // dv + nh + h)),
            pl.BlockSpec((ts, dk), lambda b, h, s: (rowblk(b, h, s), h)),
            _full(norm_g.shape),
        ],
        out_specs=pl.BlockSpec((ts, dv), lambda b, h, s: (rowblk(b, h, s), h)),
        out_shape=jax.ShapeDtypeStruct((bsz * seq, nh * dv), BF16),
        scratch_shapes=[pltpu.VMEM((dv, dk), F32)],
        compiler_params=_params(("parallel", "parallel", "arbitrary")),
        name="gla_core",
    )(y, y, y, y, bcum, norm_g)


def _gla_layer(h, g6, w_in, w_gate2, b_gate, norm_g, w_o, bsz, seq):
    d = h.shape[1]
    dk_all = w_gate2.shape[1]
    rank = w_gate2.shape[0]
    n_main = w_in.shape[1] - rank
    w_main = w_in[:, :n_main].astype(BF16)
    w_low = jnp.pad(w_in[:, n_main:], ((0, 0), (0, LANES - rank))).astype(BF16)
    w2 = jnp.pad(w_gate2, ((0, LANES - rank), (0, 0))).astype(BF16)
    y = _norm_proj(h, g6, 2, w_main, jnp.zeros((1, n_main), F32), BF16)
    bcum = _gla_gate(h, g6, w_low, w2, b_gate[None, :])
    o = _gla_core(y, bcum, norm_g[None, :], bsz, seq)
    return _proj_res(o, w_o.astype(BF16), jnp.zeros((1, d), F32), h, g6, 3)


def _head_sum(x, gmat):
    hi, mid, lo = _split3(x)
    return _dot(hi, gmat) + _dot(mid, gmat) + _dot(lo, gmat)


def _rwkv_pre_body(h_ref, hp_ref, g_ref, mu_ref, wrkv_ref, w1_ref, w2_ref, a1_ref, a2_ref,
                   g1_ref, g2_ref, vec_ref, r_ref, k_ref, v_ref, lw_ref, na_ref, nb_ref, gt_ref,
                   *, blocks_per_seq):
    i = pl.program_id(0)
    tm, d = h_ref.shape
    gain = g_ref[2:3, :]
    x = _rmsnorm(h_ref[...], gain)
    prev = _rmsnorm(hp_ref[...], gain)[7:8, :]
    prev = jnp.where(i % blocks_per_seq == 0, 0.0, prev)
    row = lax.broadcasted_iota(jnp.int32, (tm, 1), 0)
    xs = jnp.where(row == 0, prev, pltpu.roll(x, shift=1, axis=0))
    xx = xs - x

    def mix(j):
        return (x + xx * mu_ref[j:j + 1, :]).astype(BF16)

    r = _dot(mix(0), wrkv_ref[0])
    k = _dot(mix(2), wrkv_ref[1])
    v = _dot(mix(3), wrkv_ref[2])
    zw = vec_ref[0:1, :] + _dot(jnp.tanh(_dot(mix(1), w1_ref[...])).astype(BF16), w2_ref[...])
    lw = -jnp.exp(-_softplus(-zw) - 0.5)
    a = _sigmoid(vec_ref[1:2, :] + _dot(_dot(mix(4), a1_ref[...]).astype(BF16), a2_ref[...]))
    gt = _dot(_sigmoid(_dot(mix(5), g1_ref[...])).astype(BF16), g2_ref[...])
    kk = k * vec_ref[2:3, :]
    li = lax.broadcasted_iota(jnp.int32, (LANES, LANES), 0) // RWKV_HEAD
    lj = lax.broadcasted_iota(jnp.int32, (LANES, LANES), 1) // RWKV_HEAD
    gmat = jnp.where(li == lj, 1.0, 0.0).astype(BF16)
    r_ref[...] = r
    v_ref[...] = v
    lw_ref[...] = lw
    gt_ref[...] = gt
    k_ref[...] = k * (1.0 + (a - 1.0) * vec_ref[3:4, :])
    for s in range(0, d, LANES):
        kks = kk[:, s:s + LANES]
        ss = _head_sum(kks * kks, gmat)
        kkn = kks / jnp.maximum(jnp.sqrt(ss), 1e-12)
        na_ref[:, s:s + LANES] = -kkn
        nb_ref[:, s:s + LANES] = kkn * a[:, s:s + LANES]


def _rwkv_pre(h, g6, mu, wrkv, w1, w2, a1, a2, g1, g2, vec, seq):
    t, d = h.shape
    tm = TOKEN_BLOCK
    tok = pl.BlockSpec((tm, d), lambda i: (i, 0))
    ins = [h, h, g6, mu, wrkv, w1, w2, a1, a2, g1, g2, vec]
    in_specs = [tok, pl.BlockSpec((8, d), lambda i: (jnp.maximum(i * (tm // 8) - 1, 0), 0))]
    in_specs += [_full(x.shape) for x in ins[2:]]
    return pl.pallas_call(
        functools.partial(_rwkv_pre_body, blocks_per_seq=seq // tm),
        grid=(t // tm,),
        in_specs=in_specs,
        out_specs=[tok] * 7,
        out_shape=[jax.ShapeDtypeStruct((t, d), F32)] * 7,
        compiler_params=_params(("parallel",)),
        name="rwkv_pre",
    )(*ins)


def _rwkv_body(r_ref, k_ref, v_ref, lw_ref, na_ref, nb_ref, gt_ref, vec_ref, o_ref, s_ref,
               *, n_chunks):
    c = RWKV_CHUNK
    hd = RWKV_HEAD

    @pl.when(pl.program_id(2) == 0)
    def _():
        s_ref[...] = jnp.zeros_like(s_ref)

    lane = lax.broadcasted_iota(jnp.int32, (c, LANES), 1)
    head0 = lane < hd
    ri = lax.broadcasted_iota(jnp.int32, (2 * c, 2 * c), 0)
    ci = lax.broadcasted_iota(jnp.int32, (2 * c, 2 * c), 1)
    strict = ri > ci
    incl = ri >= ci
    tri = jnp.where(lax.broadcasted_iota(jnp.int32, (c, c), 0)
                    >= lax.broadcasted_iota(jnp.int32, (c, c), 1), 1.0, 0.0).astype(BF16)

    def stack(x):
        return jnp.concatenate([jnp.where(head0, x, 0.0), jnp.where(head0, 0.0, x)], axis=0)

    def chunk(ic, carry):
        r0 = pl.multiple_of(ic * c, c)
        rows = pl.ds(r0, c)
        r = r_ref[rows, :]
        k = k_ref[rows, :]
        v = v_ref[rows, :]
        lw = lw_ref[rows, :]
        na = na_ref[rows, :]
        nb = nb_ref[rows, :]
        hi, mid, lo = _split3(lw)
        cl = _dot(tri, hi) + _dot(tri, mid) + _dot(tri, lo)
        cl_last = cl[c - 1:c, :]
        p_in = jnp.exp(cl)
        p_inv = jnp.exp(-cl)
        la = stack(na * jnp.exp(cl - lw)).astype(BF16)
        lr = stack(r * p_in).astype(BF16)
        rb = stack(nb * p_inv).astype(BF16)
        rk = stack(k * p_inv).astype(BF16)
        p_end = jnp.exp(cl_last - cl)
        rbp = stack(nb * p_end).astype(BF16)
        rkp = stack(k * p_end).astype(BF16)
        vs = stack(v)
        vsb = vs.astype(BF16)
        s_bd = s_ref[...]
        sb = s_bd.astype(BF16)
        a_ab = jnp.where(strict, _dot_nt(la, rb), 0.0)
        a_ak = jnp.where(strict, _dot_nt(la, rk), 0.0).astype(BF16)
        a_rb = jnp.where(incl, _dot_nt(lr, rb), 0.0).astype(BF16)
        a_rk = jnp.where(incl, _dot_nt(lr, rk), 0.0).astype(BF16)
        u = _dot_nt(la, sb) + _dot(a_ak, vsb)
        apow = a_ab
        steps = (c - 1).bit_length()
        for it in range(steps):
            ab = apow.astype(BF16)
            u = u + _dot(ab, u.astype(BF16))
            if it + 1 < steps:
                apow = _dot(ab, ab)
        ub = u.astype(BF16)
        y = _dot_nt(lr, sb) + _dot(a_rb, ub) + _dot(a_rk, vsb)
        y = y[:c, :] + y[c:, :]
        s_ref[...] = s_bd * jnp.exp(cl_last) + _dot_tn(ub, rbp) + _dot_tn(vsb, rkp)

        def hsum(x):
            s0 = jnp.sum(jnp.where(head0, x, 0.0), axis=-1, keepdims=True)
            s1 = jnp.sum(jnp.where(head0, 0.0, x), axis=-1, keepdims=True)
            return jnp.where(head0, s0, s1)

        mean = hsum(y) * (1.0 / hd)
        yc = y - mean
        var = hsum(yc * yc) * (1.0 / hd)
        yn = yc * lax.rsqrt(var + RWKV_LNX_EPS) * vec_ref[1:2, :] + vec_ref[2:3, :]
        yn = yn + hsum(r * k * vec_ref[0:1, :]) * v
        o_ref[rows, :] = (yn * gt_ref[rows, :]).astype(o_ref.dtype)
        return carry

    lax.fori_loop(0, n_chunks, chunk, 0)


def _rwkv_core(r, k, v, lw, na, nb, gt, vec, bsz, seq):
    t, d = r.shape
    ts = RWKV_SEQ_BLOCK
    ns = seq // ts
    tok = pl.BlockSpec((ts, LANES), lambda b, p, s: (b * ns + s, p))
    return pl.pallas_call(
        functools.partial(_rwkv_body, n_chunks=ts // RWKV_CHUNK),
        grid=(bsz, d // LANES, ns),
        in_specs=[tok] * 7 + [pl.BlockSpec((vec.shape[0], LANES), lambda b, p, s: (0, p))],
        out_specs=tok,
        out_shape=jax.ShapeDtypeStruct((t, d), BF16),
        scratch_shapes=[pltpu.VMEM((LANES, LANES), F32)],
        compiler_params=_params(("parallel", "parallel", "arbitrary")),
        name="rwkv_core",
    )(r, k, v, lw, na, nb, gt, vec)


def _pad_to(x, axis, size):
    pads = [(0, 0)] * x.ndim
    pads[axis] = (0, size - x.shape[axis])
    return jnp.pad(x, pads)


def _rwkv_layer(h, g6, mu, w_rkv, w0, w1, w2, a0, a1, a2, g1, g2, k_k, k_a, r_k, lnx_g, lnx_b,
                w_o, bsz, seq):
    d = h.shape[1]
    up = lambda n: -(-n // LANES) * LANES
    w1p = _pad_to(w1, 1, up(w1.shape[1])).astype(BF16)
    w2p = _pad_to(w2, 0, up(w2.shape[0])).astype(BF16)
    a1p = _pad_to(a1, 1, up(a1.shape[1])).astype(BF16)
    a2p = _pad_to(a2, 0, up(a2.shape[0])).astype(BF16)
    g1p = _pad_to(g1, 1, up(g1.shape[1])).astype(BF16)
    g2p = _pad_to(g2, 0, up(g2.shape[0])).astype(BF16)
    vec_pre = _pad_to(jnp.stack([w0, a0, k_k, k_a]), 0, 8)
    outs = _rwkv_pre(h, g6, _pad_to(mu, 0, 8), w_rkv.astype(BF16), w1p, w2p, a1p, a2p, g1p, g2p,
                     vec_pre, seq)
    vec_core = _pad_to(jnp.stack([r_k.reshape(d), lnx_g, lnx_b]), 0, 8)
    o = _rwkv_core(*outs, vec_core, bsz, seq)
    return _proj_res(o, w_o.astype(BF16), jnp.zeros((1, d), F32), h, g6, 3)


def kernel(x, norm_g, ffn_w_in, ffn_w_out, swa_w_qkv, swa_b_qkv, swa_sinks, swa_w_o, swa_b_o, gla_w_in, gla_w_gate2, gla_b_gate, gla_norm_g, gla_w_o, rwkv_mu, rwkv_w_rkv, rwkv_w0, rwkv_w1, rwkv_w2, rwkv_a0, rwkv_a1, rwkv_a2, rwkv_g1, rwkv_g2, rwkv_k_k, rwkv_k_a, rwkv_r_k, rwkv_lnx_g, rwkv_lnx_b, rwkv_w_o):
    bsz, seq, d = x.shape
    depth = norm_g.shape[0]
    h = x.reshape(bsz * seq, d)
    for layer in range(depth):
        g6 = norm_g[layer]
        h = _ffn(h, g6, ffn_w_in[layer, 0], ffn_w_out[layer, 0], 0)
        kind, j = layer % 3, layer // 3
        if kind == 0:
            h = _swa_layer(h, g6, swa_w_qkv[j], swa_b_qkv[j], swa_sinks[j], swa_w_o[j], swa_b_o[j],
                           bsz, seq)
        elif kind == 1:
            h = _gla_layer(h, g6, gla_w_in[j], gla_w_gate2[j], gla_b_gate[j], gla_norm_g[j],
                           gla_w_o[j], bsz, seq)
        else:
            h = _rwkv_layer(h, g6, rwkv_mu[j], rwkv_w_rkv[j], rwkv_w0[j], rwkv_w1[j], rwkv_w2[j],
                            rwkv_a0[j], rwkv_a1[j], rwkv_a2[j], rwkv_g1[j], rwkv_g2[j],
                            rwkv_k_k[j], rwkv_k_a[j], rwkv_r_k[j], rwkv_lnx_g[j], rwkv_lnx_b[j],
                            rwkv_w_o[j], bsz, seq)
        h = _ffn(h, g6, ffn_w_in[layer, 1], ffn_w_out[layer, 1], 4)
    return h.reshape(bsz, seq, d)
```

```python
import functools

import jax
import jax.numpy as jnp
from jax import lax
from jax.experimental import pallas as pl
from jax.experimental.pallas import tpu as pltpu

F32 = jnp.float32
BF16 = jnp.bfloat16

NORM_EPS = 1e-6
GLA_NORM_EPS = 1e-5
GLA_GATE_TEMP = 16.0
RWKV_LNX_EPS = 64e-5

D_FF_CHUNK = 256
TOKEN_BLOCK = 512
SWA_BLOCK = 128
SWA_HEAD_DIM = 64
SWA_HEADS = 16
SWA_KV_HEADS = 4
GLA_HEADS = 4
GLA_CHUNK = 64
GLA_SEQ_BLOCK = 512
RWKV_HEAD = 64
RWKV_CHUNK = 64
RWKV_SEQ_BLOCK = 512
RWKV_PAIRS = 8
LANES = 128
VMEM_LIMIT = 56 * 1024 * 1024


def _params(sem):
    return pltpu.CompilerParams(dimension_semantics=sem, vmem_limit_bytes=VMEM_LIMIT)


def _rmsnorm(x, g, eps=NORM_EPS):
    return x * lax.rsqrt(jnp.mean(x * x, axis=-1, keepdims=True) + eps) * g


def _sigmoid(x):
    return 1.0 / (1.0 + jnp.exp(-x))


def _softplus(x):
    return jnp.maximum(x, 0.0) + jnp.log(1.0 + jnp.exp(-jnp.abs(x)))


def _dot(a, b):
    return jnp.dot(a, b, preferred_element_type=F32)


def _dot_nt(a, b):
    return lax.dot_general(a, b, (((1,), (1,)), ((), ())), preferred_element_type=F32)


def _dot_tn(a, b):
    return lax.dot_general(a, b, (((0,), (0,)), ((), ())), preferred_element_type=F32)


def _full(shape):
    n = len(shape)
    return pl.BlockSpec(shape, lambda *_: (0,) * n)


def _ffn_body(h_ref, g_ref, win_ref, wout_ref, o_ref, xn_ref, acc_ref, *, g_row, n_chunks):
    x = h_ref[...]
    xn_ref[...] = _rmsnorm(x, g_ref[g_row:g_row + 1, :]).astype(BF16)
    acc_ref[...] = jnp.zeros_like(acc_ref)

    def chunk(j, carry):
        xn = xn_ref[...]
        gate = _dot(xn, win_ref[0, j])
        up = _dot(xn, win_ref[1, j])
        act = (gate * _sigmoid(gate) * up).astype(BF16)
        acc_ref[...] += _dot(act, wout_ref[j])
        return carry

    lax.fori_loop(0, n_chunks, chunk, 0)
    o_ref[...] = x + 0.5 * _rmsnorm(acc_ref[...], g_ref[g_row + 1:g_row + 2, :])


def _ffn(h, g6, w_in, w_out, g_row):
    t, d = h.shape
    d_ff = w_out.shape[0]
    n_chunks = d_ff // D_FF_CHUNK
    win = w_in.astype(BF16).reshape(d, 2, n_chunks, D_FF_CHUNK).transpose(1, 2, 0, 3)
    wout = w_out.astype(BF16).reshape(n_chunks, D_FF_CHUNK, d)
    tm = TOKEN_BLOCK
    return pl.pallas_call(
        functools.partial(_ffn_body, g_row=g_row, n_chunks=n_chunks),
        grid=(t // tm,),
        in_specs=[
            pl.BlockSpec((tm, d), lambda i: (i, 0)),
            _full(g6.shape),
            _full(win.shape),
            _full(wout.shape),
        ],
        out_specs=pl.BlockSpec((tm, d), lambda i: (i, 0)),
        out_shape=jax.ShapeDtypeStruct((t, d), F32),
        scratch_shapes=[pltpu.VMEM((tm, d), BF16), pltpu.VMEM((tm, d), F32)],
        compiler_params=_params(("parallel",)),
        name="ffn",
    )(h, g6, win, wout)


def _norm_proj_body(h_ref, g_ref, w_ref, b_ref, o_ref, *, g_row, n_chunk):
    xn = _rmsnorm(h_ref[...], g_ref[g_row:g_row + 1, :]).astype(BF16)
    n = w_ref.shape[1]
    for c in range(0, n, n_chunk):
        y = _dot(xn, w_ref[:, c:c + n_chunk]) + b_ref[:, c:c + n_chunk]
        o_ref[:, c:c + n_chunk] = y.astype(o_ref.dtype)


def _norm_proj(h, g6, g_row, w, b, out_dtype):
    t, d = h.shape
    n = w.shape[1]
    tm = TOKEN_BLOCK
    return pl.pallas_call(
        functools.partial(_norm_proj_body, g_row=g_row, n_chunk=512),
        grid=(t // tm,),
        in_specs=[
            pl.BlockSpec((tm, d), lambda i: (i, 0)),
            _full(g6.shape),
            _full(w.shape),
            _full(b.shape),
        ],
        out_specs=pl.BlockSpec((tm, n), lambda i: (i, 0)),
        out_shape=jax.ShapeDtypeStruct((t, n), out_dtype),
        compiler_params=_params(("parallel",)),
        name="norm_proj",
    )(h, g6, w, b)


def _proj_res_body(a_ref, w_ref, b_ref, h_ref, g_ref, o_ref, *, g_row):
    y = _dot(a_ref[...], w_ref[...]) + b_ref[...]
    o_ref[...] = h_ref[...] + _rmsnorm(y, g_ref[g_row:g_row + 1, :])


def _proj_res(a, w, b, h, g6, g_row):
    t, d = h.shape
    k = a.shape[1]
    tm = TOKEN_BLOCK
    return pl.pallas_call(
        functools.partial(_proj_res_body, g_row=g_row),
        grid=(t // tm,),
        in_specs=[
            pl.BlockSpec((tm, k), lambda i: (i, 0)),
            _full(w.shape),
            _full(b.shape),
            pl.BlockSpec((tm, d), lambda i: (i, 0)),
            _full(g6.shape),
        ],
        out_specs=pl.BlockSpec((tm, d), lambda i: (i, 0)),
        out_shape=jax.ShapeDtypeStruct((t, d), F32),
        compiler_params=_params(("parallel",)),
        name="proj_res",
    )(a, w, b, h, g6)


def _swa_body(sink_ref, q_ref, kp_ref, kc_ref, vp_ref, vc_ref, o_ref):
    n = pl.program_id(1)
    blk = SWA_BLOCK
    kcat = jnp.concatenate([kp_ref[...], kc_ref[...]], axis=0)
    vcat = jnp.concatenate([vp_ref[...], vc_ref[...]], axis=0)
    q_idx = lax.broadcasted_iota(jnp.int32, (blk, 2 * blk), 0) + blk
    k_idx = lax.broadcasted_iota(jnp.int32, (blk, 2 * blk), 1)
    dist = q_idx - k_idx
    valid = (dist >= 0) & (dist < blk) & ((k_idx >= blk) | (n > 0))
    lane_head = lax.broadcasted_iota(jnp.int32, (blk, SWA_KV_HEADS * SWA_HEAD_DIM), 1) // SWA_HEAD_DIM
    scale = SWA_HEAD_DIM ** -0.5
    groups = SWA_HEADS // SWA_KV_HEADS
    width = SWA_KV_HEADS * SWA_HEAD_DIM
    heads = [(s, hh) for s in range(groups) for hh in range(SWA_KV_HEADS)]
    qm = []
    for s in range(groups):
        qs = q_ref[:, s * width:(s + 1) * width].astype(F32) * scale
        for hh in range(SWA_KV_HEADS):
            qm.append(jnp.where(lane_head == hh, qs, 0.0).astype(BF16))
    sc_all = _dot_nt(jnp.concatenate(qm, axis=0), kcat)
    probs = []
    for i, (s, hh) in enumerate(heads):
        sc = jnp.where(valid, sc_all[i * blk:(i + 1) * blk, :], -jnp.inf)
        sink = sink_ref[hh * groups + s]
        m = jnp.maximum(jnp.max(sc, axis=-1, keepdims=True), sink)
        p = jnp.exp(sc - m)
        inv = 1.0 / (jnp.sum(p, axis=-1, keepdims=True) + jnp.exp(sink - m))
        probs.append((p * inv).astype(BF16))
    o_all = _dot(jnp.concatenate(probs, axis=0), vcat)
    for s in range(groups):
        out = jnp.zeros((blk, width), F32)
        for hh in range(SWA_KV_HEADS):
            i = s * SWA_KV_HEADS + hh
            out = jnp.where(lane_head == hh, o_all[i * blk:(i + 1) * blk, :], out)
        o_ref[:, s * width:(s + 1) * width] = out.astype(o_ref.dtype)


def _swa_core(qkv, sinks, bsz, seq):
    nb = seq // SWA_BLOCK
    blk = SWA_BLOCK
    dq = SWA_HEADS * SWA_HEAD_DIM
    dkv = SWA_KV_HEADS * SWA_HEAD_DIM
    kcol = dq // dkv
    cur = lambda b, n: b * nb + n
    prev = lambda b, n: b * nb + jnp.maximum(n - 1, 0)
    return pl.pallas_call(
        _swa_body,
        grid=(bsz, nb),
        in_specs=[
            pl.BlockSpec(memory_space=pltpu.SMEM),
            pl.BlockSpec((blk, dq), lambda b, n: (cur(b, n), 0)),
            pl.BlockSpec((blk, dkv), lambda b, n: (prev(b, n), kcol)),
            pl.BlockSpec((blk, dkv), lambda b, n: (cur(b, n), kcol)),
            pl.BlockSpec((blk, dkv), lambda b, n: (prev(b, n), kcol + 1)),
            pl.BlockSpec((blk, dkv), lambda b, n: (cur(b, n), kcol + 1)),
        ],
        out_specs=pl.BlockSpec((blk, dq), lambda b, n: (cur(b, n), 0)),
        out_shape=jax.ShapeDtypeStruct((bsz * seq, dq), BF16),
        compiler_params=_params(("parallel", "parallel")),
        name="swa_core",
    )(sinks, qkv, qkv, qkv, qkv, qkv)


def _swa_layer(h, g6, w_qkv, b_qkv, sinks, w_o, b_o, bsz, seq):
    d = h.shape[1]
    hd, nh, nkv = SWA_HEAD_DIM, SWA_HEADS, SWA_KV_HEADS
    grp = nh // nkv
    dq = nh * hd
    wq = w_qkv[:, :dq].reshape(d, nkv, grp, hd).transpose(0, 2, 1, 3).reshape(d, dq)
    bq = b_qkv[:dq].reshape(nkv, grp, hd).transpose(1, 0, 2).reshape(dq)
    w = jnp.concatenate([wq, w_qkv[:, dq:]], axis=1).astype(BF16)
    b = jnp.concatenate([bq, b_qkv[dq:]])[None, :]
    wo = w_o.reshape(nkv, grp, hd, d).transpose(1, 0, 2, 3).reshape(dq, d).astype(BF16)
    qkv = _norm_proj(h, g6, 2, w, b, BF16)
    att = _swa_core(qkv, sinks, bsz, seq)
    return _proj_res(att, wo, b_o[None, :], h, g6, 3)


def _split3(x):
    hi = x.astype(BF16)
    r1 = x - hi.astype(F32)
    mid = r1.astype(BF16)
    lo = (r1 - mid.astype(F32)).astype(BF16)
    return hi, mid, lo


def _gla_gate_body(h_ref, g_ref, wl_ref, w2_ref, bg_ref, o_ref):
    tm = h_ref.shape[0]
    xn = _rmsnorm(h_ref[...], g_ref[2:3, :]).astype(BF16)
    g_low = _dot(xn, wl_ref[...]).astype(BF16)
    z = _dot(g_low, w2_ref[...]) + bg_ref[...]
    la = -_softplus(-z) * (1.0 / GLA_GATE_TEMP)
    row = lax.broadcasted_iota(jnp.int32, (tm, tm), 0)
    col = lax.broadcasted_iota(jnp.int32, (tm, tm), 1)
    tri = jnp.where((row >= col) & (row // GLA_CHUNK == col // GLA_CHUNK), 1.0, 0.0).astype(BF16)
    hi, mid, lo = _split3(la)
    o_ref[...] = _dot(tri, hi) + _dot(tri, mid) + _dot(tri, lo)


def _gla_gate(h, g6, w_low, w_gate2, b_gate):
    t, d = h.shape
    dk = w_gate2.shape[1]
    tm = TOKEN_BLOCK
    return pl.pallas_call(
        _gla_gate_body,
        grid=(t // tm,),
        in_specs=[
            pl.BlockSpec((tm, d), lambda i: (i, 0)),
            _full(g6.shape),
            _full(w_low.shape),
            _full(w_gate2.shape),
            _full(b_gate.shape),
        ],
        out_specs=pl.BlockSpec((tm, dk), lambda i: (i, 0)),
        out_shape=jax.ShapeDtypeStruct((t, dk), F32),
        compiler_params=_params(("parallel",)),
        name="gla_gate",
    )(h, g6, w_low, w_gate2, b_gate)


def _block_first(x, s):
    c, l = x.shape
    xb = x.reshape(c // s, s, l)
    return jnp.broadcast_to(xb[:, 0:1, :], (c // s, s, l)).reshape(c, l)


def _gla_body(q_ref, k_ref, v_ref, go_ref, b_ref, ng_ref, o_ref, st_ref, *, n_chunks):
    c = GLA_CHUNK
    dk = q_ref.shape[1]

    @pl.when(pl.program_id(2) == 0)
    def _():
        st_ref[...] = jnp.zeros_like(st_ref)

    row = lax.broadcasted_iota(jnp.int32, (c, 1), 0)
    ri = lax.broadcasted_iota(jnp.int32, (c, c), 0)
    ci = lax.broadcasted_iota(jnp.int32, (c, c), 1)

    def chunk(ic, carry):
        r0 = pl.multiple_of(ic * c, c)
        q = q_ref[pl.ds(r0, c), :].astype(F32) * (dk ** -0.5)
        k = k_ref[pl.ds(r0, c), :].astype(F32)
        v = v_ref[pl.ds(r0, c), :]
        b = b_ref[pl.ds(r0, c), :]
        st = st_ref[...]
        o = _dot_nt((q * jnp.exp(b)).astype(BF16), st.astype(BF16))
        a = jnp.zeros((c, c), F32)
        for s in (8, 16, 32):
            first = _block_first(b, s)
            nxt = jnp.concatenate([first[s:], first[:s]], axis=0)
            odd = (row // s) % 2 == 1
            qh = jnp.where(odd, q * jnp.exp(jnp.where(odd, b - first, 0.0)), 0.0)
            kh = jnp.where(odd, 0.0, k * jnp.exp(jnp.where(odd, 0.0, nxt - b)))
            al = _dot_nt(qh.astype(BF16), kh.astype(BF16))
            a = a + jnp.where(ri // (2 * s) == ci // (2 * s), al, 0.0)
        k3 = k.reshape(c // 8, 8, dk)
        b3 = b.reshape(c // 8, 8, dk)
        for jj in range(8):
            kj = jnp.broadcast_to(k3[:, jj:jj + 1, :], k3.shape).reshape(c, dk)
            bj = jnp.broadcast_to(b3[:, jj:jj + 1, :], b3.shape).reshape(c, dk)
            keep = row % 8 >= jj
            tt = q * kj * jnp.exp(jnp.where(keep, b - bj, 0.0))
            colv = jnp.sum(tt, axis=-1, keepdims=True)
            hit = (ci == 8 * (ri // 8) + jj) & (ri % 8 >= jj)
            a = a + jnp.where(hit, colv, 0.0)
        o = o + _dot(a.astype(BF16), v)
        o = _rmsnorm(o, ng_ref[...], GLA_NORM_EPS)
        go = go_ref[pl.ds(r0, c), :].astype(F32)
        o_ref[pl.ds(r0, c), :] = (o * (go * _sigmoid(go))).astype(o_ref.dtype)
        b_last = b[c - 1:c, :]
        kd = (k * jnp.exp(b_last - b)).astype(BF16)
        st_ref[...] = st * jnp.exp(b_last) + _dot_tn(v, kd)
        return carry

    lax.fori_loop(0, n_chunks, chunk, 0)


def _gla_core(y, bcum, norm_g, bsz, seq):
    nh = GLA_HEADS
    dk = bcum.shape[1] // nh
    dv = (y.shape[1] - 2 * nh * dk) // 2 // nh
    ts = GLA_SEQ_BLOCK
    ns = seq // ts
    rowblk = lambda b, h, s: b * ns + s
    kq = nh * dk // dk
    return pl.pallas_call(
        functools.partial(_gla_body, n_chunks=ts // GLA_CHUNK),
        grid=(bsz, nh, ns),
        in_specs=[
            pl.BlockSpec((ts, dk), lambda b, h, s: (rowblk(b, h, s), h)),
            pl.BlockSpec((ts, dk), lambda b, h, s: (rowblk(b, h, s), kq + h)),
            pl.BlockSpec((ts, dv), lambda b, h, s: (rowblk(b, h, s), (2 * nh * dk) // dv + h)),
            pl.BlockSpec((ts, dv), lambda b, h, s: (rowblk(b, h, s), (2 * nh * dk) // dv + nh + h)),
            pl.BlockSpec((ts, dk), lambda b, h, s: (rowblk(b, h, s), h)),
            _full(norm_g.shape),
        ],
        out_specs=pl.BlockSpec((ts, dv), lambda b, h, s: (rowblk(b, h, s), h)),
        out_shape=jax.ShapeDtypeStruct((bsz * seq, nh * dv), BF16),
        scratch_shapes=[pltpu.VMEM((dv, dk), F32)],
        compiler_params=_params(("parallel", "parallel", "arbitrary")),
        name="gla_core",
    )(y, y, y, y, bcum, norm_g)


def _gla_layer(h, g6, w_in, w_gate2, b_gate, norm_g, w_o, bsz, seq):
    d = h.shape[1]
    dk_all = w_gate2.shape[1]
    rank = w_gate2.shape[0]
    n_main = w_in.shape[1] - rank
    w_main = w_in[:, :n_main].astype(BF16)
    w_low = jnp.pad(w_in[:, n_main:], ((0, 0), (0, LANES - rank))).astype(BF16)
    w2 = jnp.pad(w_gate2, ((0, LANES - rank), (0, 0))).astype(BF16)
    y = _norm_proj(h, g6, 2, w_main, jnp.zeros((1, n_main), F32), BF16)
    bcum = _gla_gate(h, g6, w_low, w2, b_gate[None, :])
    o = _gla_core(y, bcum, norm_g[None, :], bsz, seq)
    return _proj_res(o, w_o.astype(BF16), jnp.zeros((1, d), F32), h, g6, 3)


def _head_sum(x, gmat):
    hi, mid, lo = _split3(x)
    return _dot(hi, gmat) + _dot(mid, gmat) + _dot(lo, gmat)


def _rwkv_pre_body(h_ref, hp_ref, g_ref, mu_ref, wrkv_ref, w1_ref, w2_ref, a1_ref, a2_ref,
                   g1_ref, g2_ref, vec_ref, r_ref, k_ref, v_ref, lw_ref, na_ref, nb_ref, gt_ref,
                   *, blocks_per_seq):
    i = pl.program_id(0)
    tm, d = h_ref.shape
    gain = g_ref[2:3, :]
    x = _rmsnorm(h_ref[...], gain)
    prev = _rmsnorm(hp_ref[...], gain)[7:8, :]
    prev = jnp.where(i % blocks_per_seq == 0, 0.0, prev)
    row = lax.broadcasted_iota(jnp.int32, (tm, 1), 0)
    xs = jnp.where(row == 0, prev, pltpu.roll(x, shift=1, axis=0))
    xx = xs - x

    def mix(j):
        return (x + xx * mu_ref[j:j + 1, :]).astype(BF16)

    r = _dot(mix(0), wrkv_ref[0])
    k = _dot(mix(2), wrkv_ref[1])
    v = _dot(mix(3), wrkv_ref[2])
    zw = vec_ref[0:1, :] + _dot(jnp.tanh(_dot(mix(1), w1_ref[...])).astype(BF16), w2_ref[...])
    lw = -jnp.exp(-_softplus(-zw) - 0.5)
    a = _sigmoid(vec_ref[1:2, :] + _dot(_dot(mix(4), a1_ref[...]).astype(BF16), a2_ref[...]))
    gt = _dot(_sigmoid(_dot(mix(5), g1_ref[...])).astype(BF16), g2_ref[...])
    kk = k * vec_ref[2:3, :]
    li = lax.broadcasted_iota(jnp.int32, (LANES, LANES), 0) // RWKV_HEAD
    lj = lax.broadcasted_iota(jnp.int32, (LANES, LANES), 1) // RWKV_HEAD
    gmat = jnp.where(li == lj, 1.0, 0.0).astype(BF16)
    r_ref[...] = r
    v_ref[...] = v
    lw_ref[...] = lw
    gt_ref[...] = gt
    k_ref[...] = k * (1.0 + (a - 1.0) * vec_ref[3:4, :])
    for s in range(0, d, LANES):
        kks = kk[:, s:s + LANES]
        ss = _head_sum(kks * kks, gmat)
        kkn = kks / jnp.maximum(jnp.sqrt(ss), 1e-12)
        na_ref[:, s:s + LANES] = -kkn
        nb_ref[:, s:s + LANES] = kkn * a[:, s:s + LANES]


def _rwkv_pre(h, g6, mu, wrkv, w1, w2, a1, a2, g1, g2, vec, seq):
    t, d = h.shape
    tm = TOKEN_BLOCK
    tok = pl.BlockSpec((tm, d), lambda i: (i, 0))
    ins = [h, h, g6, mu, wrkv, w1, w2, a1, a2, g1, g2, vec]
    in_specs = [tok, pl.BlockSpec((8, d), lambda i: (jnp.maximum(i * (tm // 8) - 1, 0), 0))]
    in_specs += [_full(x.shape) for x in ins[2:]]
    return pl.pallas_call(
        functools.partial(_rwkv_pre_body, blocks_per_seq=seq // tm),
        grid=(t // tm,),
        in_specs=in_specs,
        out_specs=[tok] * 7,
        out_shape=[jax.ShapeDtypeStruct((t, d), F32)] * 7,
        compiler_params=_params(("parallel",)),
        name="rwkv_pre",
    )(*ins)


def _rwkv_body(r_ref, k_ref, v_ref, lw_ref, na_ref, nb_ref, gt_ref, vec_ref, o_ref, s_ref,
               *, n_chunks, n_pairs):
    c = RWKV_CHUNK
    hd = RWKV_HEAD
    pairs = range(n_pairs)

    @pl.when(pl.program_id(2) == 0)
    def _():
        s_ref[...] = jnp.zeros_like(s_ref)

    lane = lax.broadcasted_iota(jnp.int32, (c, LANES), 1)
    head0 = lane < hd
    ri = lax.broadcasted_iota(jnp.int32, (2 * c, 2 * c), 0)
    ci = lax.broadcasted_iota(jnp.int32, (2 * c, 2 * c), 1)
    strict = ri > ci
    incl = ri >= ci
    tri = jnp.where(lax.broadcasted_iota(jnp.int32, (c, c), 0)
                    >= lax.broadcasted_iota(jnp.int32, (c, c), 1), 1.0, 0.0).astype(BF16)

    def stack(x):
        return jnp.concatenate([jnp.where(head0, x, 0.0), jnp.where(head0, 0.0, x)], axis=0)

    def hsum(x):
        s0 = jnp.sum(jnp.where(head0, x, 0.0), axis=-1, keepdims=True)
        s1 = jnp.sum(jnp.where(head0, 0.0, x), axis=-1, keepdims=True)
        return jnp.where(head0, s0, s1)

    def chunk(ic, carry):
        r0 = pl.multiple_of(ic * c, c)
        rows = pl.ds(r0, c)
        hi, mid, lo = _split3(lw_ref[rows, :])
        cl_all = _dot(tri, hi) + _dot(tri, mid) + _dot(tri, lo)
        lhs, rhs, rhs_end, vsb, sb, decay = [], [], [], [], [], []
        for p in pairs:
            sl = slice(p * LANES, (p + 1) * LANES)
            r, k, v = r_ref[rows, sl], k_ref[rows, sl], v_ref[rows, sl]
            lw, na, nb = lw_ref[rows, sl], na_ref[rows, sl], nb_ref[rows, sl]
            cl = cl_all[:, sl]
            cl_last = cl[c - 1:c, :]
            p_inv = jnp.exp(-cl)
            p_end = jnp.exp(cl_last - cl)
            lhs.append(jnp.concatenate([stack(na * jnp.exp(cl - lw)), stack(r * jnp.exp(cl))],
                                       axis=0).astype(BF16))
            rhs.append(jnp.concatenate([stack(nb * p_inv), stack(k * p_inv)], axis=0).astype(BF16))
            rhs_end.append(jnp.concatenate([stack(nb * p_end), stack(k * p_end)], axis=0).astype(BF16))
            vsb.append(stack(v).astype(BF16))
            sb.append(s_ref[p].astype(BF16))
            decay.append(jnp.exp(cl_last))
        gram = [_dot_nt(lhs[p], rhs[p]) for p in pairs]
        ws = [_dot_nt(lhs[p], sb[p]) for p in pairs]
        a_ab = [jnp.where(strict, gram[p][:2 * c, :2 * c], 0.0) for p in pairs]
        a_ak = [jnp.where(strict, gram[p][:2 * c, 2 * c:], 0.0).astype(BF16) for p in pairs]
        a_r = [jnp.concatenate([jnp.where(incl, gram[p][2 * c:, :2 * c], 0.0),
                                jnp.where(incl, gram[p][2 * c:, 2 * c:], 0.0)], axis=1).astype(BF16)
               for p in pairs]
        u = [ws[p][:2 * c] + _dot(a_ak[p], vsb[p]) for p in pairs]
        apow = a_ab
        steps = (c - 1).bit_length()
        for it in range(steps):
            ab = [apow[p].astype(BF16) for p in pairs]
            u = [u[p] + _dot(ab[p], u[p].astype(BF16)) for p in pairs]
            if it + 1 < steps:
                apow = [_dot(ab[p], ab[p]) for p in pairs]
        uv = [jnp.concatenate([u[p].astype(BF16), vsb[p]], axis=0) for p in pairs]
        ys = [ws[p][2 * c:] + _dot(a_r[p], uv[p]) for p in pairs]
        for p in pairs:
            s_ref[p] = s_ref[p] * decay[p] + _dot_tn(uv[p], rhs_end[p])
        for p in pairs:
            sl = slice(p * LANES, (p + 1) * LANES)
            y = ys[p][:c, :] + ys[p][c:, :]
            mean = hsum(y) * (1.0 / hd)
            yc = y - mean
            var = hsum(yc * yc) * (1.0 / hd)
            yn = yc * lax.rsqrt(var + RWKV_LNX_EPS) * vec_ref[1:2, sl] + vec_ref[2:3, sl]
            yn = yn + hsum(r_ref[rows, sl] * k_ref[rows, sl] * vec_ref[0:1, sl]) * v_ref[rows, sl]
            o_ref[rows, sl] = (yn * gt_ref[rows, sl]).astype(o_ref.dtype)
        return carry

    lax.fori_loop(0, n_chunks, chunk, 0)


def _rwkv_core(r, k, v, lw, na, nb, gt, vec, bsz, seq):
    t, d = r.shape
    ts = RWKV_SEQ_BLOCK
    ns = seq // ts
    width = RWKV_PAIRS * LANES
    tok = pl.BlockSpec((ts, width), lambda b, p, s: (b * ns + s, p))
    return pl.pallas_call(
        functools.partial(_rwkv_body, n_chunks=ts // RWKV_CHUNK, n_pairs=RWKV_PAIRS),
        grid=(bsz, d // width, ns),
        in_specs=[tok] * 7 + [pl.BlockSpec((vec.shape[0], width), lambda b, p, s: (0, p))],
        out_specs=tok,
        out_shape=jax.ShapeDtypeStruct((t, d), BF16),
        scratch_shapes=[pltpu.VMEM((RWKV_PAIRS, LANES, LANES), F32)],
        compiler_params=_params(("parallel", "parallel", "arbitrary")),
        name="rwkv_core",
    )(r, k, v, lw, na, nb, gt, vec)


def _pad_to(x, axis, size):
    pads = [(0, 0)] * x.ndim
    pads[axis] = (0, size - x.shape[axis])
    return jnp.pad(x, pads)


def _rwkv_layer(h, g6, mu, w_rkv, w0, w1, w2, a0, a1, a2, g1, g2, k_k, k_a, r_k, lnx_g, lnx_b,
                w_o, bsz, seq):
    d = h.shape[1]
    up = lambda n: -(-n // LANES) * LANES
    w1p = _pad_to(w1, 1, up(w1.shape[1])).astype(BF16)
    w2p = _pad_to(w2, 0, up(w2.shape[0])).astype(BF16)
    a1p = _pad_to(a1, 1, up(a1.shape[1])).astype(BF16)
    a2p = _pad_to(a2, 0, up(a2.shape[0])).astype(BF16)
    g1p = _pad_to(g1, 1, up(g1.shape[1])).astype(BF16)
    g2p = _pad_to(g2, 0, up(g2.shape[0])).astype(BF16)
    vec_pre = _pad_to(jnp.stack([w0, a0, k_k, k_a]), 0, 8)
    outs = _rwkv_pre(h, g6, _pad_to(mu, 0, 8), w_rkv.astype(BF16), w1p, w2p, a1p, a2p, g1p, g2p,
                     vec_pre, seq)
    vec_core = _pad_to(jnp.stack([r_k.reshape(d), lnx_g, lnx_b]), 0, 8)
    o = _rwkv_core(*outs, vec_core, bsz, seq)
    return _proj_res(o, w_o.astype(BF16), jnp.zeros((1, d), F32), h, g6, 3)


def kernel(x, norm_g, ffn_w_in, ffn_w_out, swa_w_qkv, swa_b_qkv, swa_sinks, swa_w_o, swa_b_o, gla_w_in, gla_w_gate2, gla_b_gate, gla_norm_g, gla_w_o, rwkv_mu, rwkv_w_rkv, rwkv_w0, rwkv_w1, rwkv_w2, rwkv_a0, rwkv_a1, rwkv_a2, rwkv_g1, rwkv_g2, rwkv_k_k, rwkv_k_a, rwkv_r_k, rwkv_lnx_g, rwkv_lnx_b, rwkv_w_o):
    bsz, seq, d = x.shape
    depth = norm_g.shape[0]
    h = x.reshape(bsz * seq, d)
    for layer in range(depth):
        g6 = norm_g[layer]
        h = _ffn(h, g6, ffn_w_in[layer, 0], ffn_w_out[layer, 0], 0)
        kind, j = layer % 3, layer // 3
        if kind == 0:
            h = _swa_layer(h, g6, swa_w_qkv[j], swa_b_qkv[j], swa_sinks[j], swa_w_o[j], swa_b_o[j],
                           bsz, seq)
        elif kind == 1:
            h = _gla_layer(h, g6, gla_w_in[j], gla_w_gate2[j], gla_b_gate[j], gla_norm_g[j],
                           gla_w_o[j], bsz, seq)
        else:
            h = _rwkv_layer(h, g6, rwkv_mu[j], rwkv_w_rkv[j], rwkv_w0[j], rwkv_w1[j], rwkv_w2[j],
                            rwkv_a0[j], rwkv_a1[j], rwkv_a2[j], rwkv_g1[j], rwkv_g2[j],
                            rwkv_k_k[j], rwkv_k_a[j], rwkv_r_k[j], rwkv_lnx_g[j], rwkv_lnx_b[j],
                            rwkv_w_o[j], bsz, seq)
        h = _ffn(h, g6, ffn_w_in[layer, 1], ffn_w_out[layer, 1], 4)
    return h.reshape(bsz, seq, d)
```

```python
import functools

import jax
import jax.numpy as jnp
from jax import lax
from jax.experimental import pallas as pl
from jax.experimental.pallas import tpu as pltpu

F32 = jnp.float32
BF16 = jnp.bfloat16

NORM_EPS = 1e-6
GLA_NORM_EPS = 1e-5
GLA_GATE_TEMP = 16.0
RWKV_LNX_EPS = 64e-5

D_FF_CHUNK = 256
TOKEN_BLOCK = 512
FFN_TOKEN_BLOCK = 1024
FFN_SUB_ROWS = 512
SWA_BLOCK = 128
SWA_HEAD_DIM = 64
SWA_HEADS = 16
SWA_KV_HEADS = 4
GLA_HEADS = 4
GLA_CHUNK = 64
GLA_SEQ_BLOCK = 512
RWKV_HEAD = 64
RWKV_CHUNK = 64
RWKV_SEQ_BLOCK = 512
RWKV_PAIRS = 8
LANES = 128
VMEM_LIMIT = 56 * 1024 * 1024


def _params(sem):
    return pltpu.CompilerParams(dimension_semantics=sem, vmem_limit_bytes=VMEM_LIMIT)


def _rmsnorm(x, g, eps=NORM_EPS):
    return x * lax.rsqrt(jnp.mean(x * x, axis=-1, keepdims=True) + eps) * g


def _sigmoid(x):
    return 1.0 / (1.0 + jnp.exp(-x))


def _softplus(x):
    return jnp.maximum(x, 0.0) + jnp.log(1.0 + jnp.exp(-jnp.abs(x)))


def _dot(a, b):
    return jnp.dot(a, b, preferred_element_type=F32)


def _dot_nt(a, b):
    return lax.dot_general(a, b, (((1,), (1,)), ((), ())), preferred_element_type=F32)


def _dot_tn(a, b):
    return lax.dot_general(a, b, (((0,), (0,)), ((), ())), preferred_element_type=F32)


def _full(shape):
    n = len(shape)
    return pl.BlockSpec(shape, lambda *_: (0,) * n)


def _ffn_body(h_ref, g_ref, win_ref, wout_ref, o_ref, xn_ref, acc_ref, *, g_row):
    d_ff = wout_ref.shape[0]
    tf = D_FF_CHUNK
    tm = h_ref.shape[0]
    subs = [slice(r0, r0 + FFN_SUB_ROWS) for r0 in range(0, tm, FFN_SUB_ROWS)]
    for rs in subs:
        xn_ref[rs, :] = _rmsnorm(h_ref[rs, :], g_ref[g_row:g_row + 1, :]).astype(BF16)
    for rs in subs:
        for c0 in range(0, d_ff, tf):
            xn = xn_ref[rs, :]
            gate = _dot(xn, win_ref[:, c0:c0 + tf])
            up = _dot(xn, win_ref[:, d_ff + c0:d_ff + c0 + tf])
            act = (gate * _sigmoid(gate) * up).astype(BF16)
            part = _dot(act, wout_ref[c0:c0 + tf, :])
            if c0 == 0:
                acc_ref[rs, :] = part
            else:
                acc_ref[rs, :] += part
    for rs in subs:
        o_ref[rs, :] = h_ref[rs, :] + 0.5 * _rmsnorm(acc_ref[rs, :], g_ref[g_row + 1:g_row + 2, :])


def _resident(shape):
    n = len(shape)
    return pl.BlockSpec(shape, lambda *_: (0,) * n, pipeline_mode=pl.Buffered(1))


def _ffn(h, g6, w_in, w_out, g_row):
    t, d = h.shape
    win = w_in.astype(BF16)
    wout = w_out.astype(BF16)
    tm = FFN_TOKEN_BLOCK
    return pl.pallas_call(
        functools.partial(_ffn_body, g_row=g_row),
        grid=(t // tm,),
        in_specs=[
            pl.BlockSpec((tm, d), lambda i: (i, 0)),
            _full(g6.shape),
            _resident(win.shape),
            _resident(wout.shape),
        ],
        out_specs=pl.BlockSpec((tm, d), lambda i: (i, 0)),
        out_shape=jax.ShapeDtypeStruct((t, d), F32),
        scratch_shapes=[pltpu.VMEM((tm, d), BF16), pltpu.VMEM((tm, d), F32)],
        compiler_params=_params(("parallel",)),
        name="ffn",
    )(h, g6, win, wout)


def _norm_proj_body(h_ref, g_ref, w_ref, b_ref, o_ref, *, g_row, n_chunk):
    xn = _rmsnorm(h_ref[...], g_ref[g_row:g_row + 1, :]).astype(BF16)
    n = w_ref.shape[1]
    for c in range(0, n, n_chunk):
        y = _dot(xn, w_ref[:, c:c + n_chunk]) + b_ref[:, c:c + n_chunk]
        o_ref[:, c:c + n_chunk] = y.astype(o_ref.dtype)


def _norm_proj(h, g6, g_row, w, b, out_dtype):
    t, d = h.shape
    n = w.shape[1]
    tm = TOKEN_BLOCK
    return pl.pallas_call(
        functools.partial(_norm_proj_body, g_row=g_row, n_chunk=512),
        grid=(t // tm,),
        in_specs=[
            pl.BlockSpec((tm, d), lambda i: (i, 0)),
            _full(g6.shape),
            _full(w.shape),
            _full(b.shape),
        ],
        out_specs=pl.BlockSpec((tm, n), lambda i: (i, 0)),
        out_shape=jax.ShapeDtypeStruct((t, n), out_dtype),
        compiler_params=_params(("parallel",)),
        name="norm_proj",
    )(h, g6, w, b)


def _proj_res_body(a_ref, w_ref, b_ref, h_ref, g_ref, o_ref, *, g_row):
    y = _dot(a_ref[...], w_ref[...]) + b_ref[...]
    o_ref[...] = h_ref[...] + _rmsnorm(y, g_ref[g_row:g_row + 1, :])


def _proj_res(a, w, b, h, g6, g_row):
    t, d = h.shape
    k = a.shape[1]
    tm = TOKEN_BLOCK
    return pl.pallas_call(
        functools.partial(_proj_res_body, g_row=g_row),
        grid=(t // tm,),
        in_specs=[
            pl.BlockSpec((tm, k), lambda i: (i, 0)),
            _full(w.shape),
            _full(b.shape),
            pl.BlockSpec((tm, d), lambda i: (i, 0)),
            _full(g6.shape),
        ],
        out_specs=pl.BlockSpec((tm, d), lambda i: (i, 0)),
        out_shape=jax.ShapeDtypeStruct((t, d), F32),
        compiler_params=_params(("parallel",)),
        name="proj_res",
    )(a, w, b, h, g6)


def _swa_body(sink_ref, q_ref, kp_ref, kc_ref, vp_ref, vc_ref, o_ref):
    n = pl.program_id(1)
    blk = SWA_BLOCK
    kcat = jnp.concatenate([kp_ref[...], kc_ref[...]], axis=0)
    vcat = jnp.concatenate([vp_ref[...], vc_ref[...]], axis=0)
    q_idx = lax.broadcasted_iota(jnp.int32, (blk, 2 * blk), 0) + blk
    k_idx = lax.broadcasted_iota(jnp.int32, (blk, 2 * blk), 1)
    dist = q_idx - k_idx
    valid = (dist >= 0) & (dist < blk) & ((k_idx >= blk) | (n > 0))
    lane_head = lax.broadcasted_iota(jnp.int32, (blk, SWA_KV_HEADS * SWA_HEAD_DIM), 1) // SWA_HEAD_DIM
    scale = SWA_HEAD_DIM ** -0.5
    groups = SWA_HEADS // SWA_KV_HEADS
    width = SWA_KV_HEADS * SWA_HEAD_DIM
    heads = [(s, hh) for s in range(groups) for hh in range(SWA_KV_HEADS)]
    qm = []
    for s in range(groups):
        qs = q_ref[:, s * width:(s + 1) * width].astype(F32) * scale
        for hh in range(SWA_KV_HEADS):
            qm.append(jnp.where(lane_head == hh, qs, 0.0).astype(BF16))
    sc_all = _dot_nt(jnp.concatenate(qm, axis=0), kcat)
    probs = []
    for i, (s, hh) in enumerate(heads):
        sc = jnp.where(valid, sc_all[i * blk:(i + 1) * blk, :], -jnp.inf)
        sink = sink_ref[hh * groups + s]
        m = jnp.maximum(jnp.max(sc, axis=-1, keepdims=True), sink)
        p = jnp.exp(sc - m)
        inv = 1.0 / (jnp.sum(p, axis=-1, keepdims=True) + jnp.exp(sink - m))
        probs.append((p * inv).astype(BF16))
    o_all = _dot(jnp.concatenate(probs, axis=0), vcat)
    for s in range(groups):
        out = jnp.zeros((blk, width), F32)
        for hh in range(SWA_KV_HEADS):
            i = s * SWA_KV_HEADS + hh
            out = jnp.where(lane_head == hh, o_all[i * blk:(i + 1) * blk, :], out)
        o_ref[:, s * width:(s + 1) * width] = out.astype(o_ref.dtype)


def _swa_core(qkv, sinks, bsz, seq):
    nb = seq // SWA_BLOCK
    blk = SWA_BLOCK
    dq = SWA_HEADS * SWA_HEAD_DIM
    dkv = SWA_KV_HEADS * SWA_HEAD_DIM
    kcol = dq // dkv
    cur = lambda b, n: b * nb + n
    prev = lambda b, n: b * nb + jnp.maximum(n - 1, 0)
    return pl.pallas_call(
        _swa_body,
        grid=(bsz, nb),
        in_specs=[
            pl.BlockSpec(memory_space=pltpu.SMEM),
            pl.BlockSpec((blk, dq), lambda b, n: (cur(b, n), 0)),
            pl.BlockSpec((blk, dkv), lambda b, n: (prev(b, n), kcol)),
            pl.BlockSpec((blk, dkv), lambda b, n: (cur(b, n), kcol)),
            pl.BlockSpec((blk, dkv), lambda b, n: (prev(b, n), kcol + 1)),
            pl.BlockSpec((blk, dkv), lambda b, n: (cur(b, n), kcol + 1)),
        ],
        out_specs=pl.BlockSpec((blk, dq), lambda b, n: (cur(b, n), 0)),
        out_shape=jax.ShapeDtypeStruct((bsz * seq, dq), BF16),
        compiler_params=_params(("parallel", "parallel")),
        name="swa_core",
    )(sinks, qkv, qkv, qkv, qkv, qkv)


def _swa_layer(h, g6, w_qkv, b_qkv, sinks, w_o, b_o, bsz, seq):
    d = h.shape[1]
    hd, nh, nkv = SWA_HEAD_DIM, SWA_HEADS, SWA_KV_HEADS
    grp = nh // nkv
    dq = nh * hd
    wq = w_qkv[:, :dq].reshape(d, nkv, grp, hd).transpose(0, 2, 1, 3).reshape(d, dq)
    bq = b_qkv[:dq].reshape(nkv, grp, hd).transpose(1, 0, 2).reshape(dq)
    w = jnp.concatenate([wq, w_qkv[:, dq:]], axis=1).astype(BF16)
    b = jnp.concatenate([bq, b_qkv[dq:]])[None, :]
    wo = w_o.reshape(nkv, grp, hd, d).transpose(1, 0, 2, 3).reshape(dq, d).astype(BF16)
    qkv = _norm_proj(h, g6, 2, w, b, BF16)
    att = _swa_core(qkv, sinks, bsz, seq)
    return _proj_res(att, wo, b_o[None, :], h, g6, 3)


def _split3(x):
    hi = x.astype(BF16)
    r1 = x - hi.astype(F32)
    mid = r1.astype(BF16)
    lo = (r1 - mid.astype(F32)).astype(BF16)
    return hi, mid, lo


def _gla_gate_body(h_ref, g_ref, wl_ref, w2_ref, bg_ref, o_ref):
    tm = h_ref.shape[0]
    xn = _rmsnorm(h_ref[...], g_ref[2:3, :]).astype(BF16)
    g_low = _dot(xn, wl_ref[...]).astype(BF16)
    z = _dot(g_low, w2_ref[...]) + bg_ref[...]
    la = -_softplus(-z) * (1.0 / GLA_GATE_TEMP)
    row = lax.broadcasted_iota(jnp.int32, (tm, tm), 0)
    col = lax.broadcasted_iota(jnp.int32, (tm, tm), 1)
    tri = jnp.where((row >= col) & (row // GLA_CHUNK == col // GLA_CHUNK), 1.0, 0.0).astype(BF16)
    hi, mid, lo = _split3(la)
    o_ref[...] = _dot(tri, hi) + _dot(tri, mid) + _dot(tri, lo)


def _gla_gate(h, g6, w_low, w_gate2, b_gate):
    t, d = h.shape
    dk = w_gate2.shape[1]
    tm = TOKEN_BLOCK
    return pl.pallas_call(
        _gla_gate_body,
        grid=(t // tm,),
        in_specs=[
            pl.BlockSpec((tm, d), lambda i: (i, 0)),
            _full(g6.shape),
            _full(w_low.shape),
            _full(w_gate2.shape),
            _full(b_gate.shape),
        ],
        out_specs=pl.BlockSpec((tm, dk), lambda i: (i, 0)),
        out_shape=jax.ShapeDtypeStruct((t, dk), F32),
        compiler_params=_params(("parallel",)),
        name="gla_gate",
    )(h, g6, w_low, w_gate2, b_gate)


def _block_first(x, s):
    c, l = x.shape
    xb = x.reshape(c // s, s, l)
    return jnp.broadcast_to(xb[:, 0:1, :], (c // s, s, l)).reshape(c, l)


def _gla_body(q_ref, k_ref, v_ref, go_ref, b_ref, ng_ref, o_ref, st_ref, *, n_chunks, n_heads):
    c = GLA_CHUNK
    dk = q_ref.shape[1] // n_heads
    dv = v_ref.shape[1] // n_heads
    heads = range(n_heads)

    @pl.when(pl.program_id(1) == 0)
    def _():
        st_ref[...] = jnp.zeros_like(st_ref)

    row = lax.broadcasted_iota(jnp.int32, (c, 1), 0)
    ri = lax.broadcasted_iota(jnp.int32, (c, c), 0)
    ci = lax.broadcasted_iota(jnp.int32, (c, c), 1)
    levels = (8, 16, 32)

    def chunk(ic, carry):
        rows = pl.ds(pl.multiple_of(ic * c, c), c)
        q, k, v, b = [], [], [], []
        for h in heads:
            q.append(q_ref[rows, h * dk:(h + 1) * dk].astype(F32) * (dk ** -0.5))
            k.append(k_ref[rows, h * dk:(h + 1) * dk].astype(F32))
            v.append(v_ref[rows, h * dv:(h + 1) * dv])
            b.append(b_ref[rows, h * dk:(h + 1) * dk])
        st = [st_ref[h] for h in heads]
        o = [_dot_nt((q[h] * jnp.exp(b[h])).astype(BF16), st[h].astype(BF16)) for h in heads]
        for h in heads:
            b_last = b[h][c - 1:c, :]
            kd = (k[h] * jnp.exp(b_last - b[h])).astype(BF16)
            st_ref[h] = st[h] * jnp.exp(b_last) + _dot_tn(v[h], kd)
        al = []
        for h in heads:
            for s in levels:
                first = _block_first(b[h], s)
                nxt = jnp.concatenate([first[s:], first[:s]], axis=0)
                odd = (row // s) % 2 == 1
                qh = jnp.where(odd, q[h] * jnp.exp(jnp.where(odd, b[h] - first, 0.0)), 0.0)
                kh = jnp.where(odd, 0.0, k[h] * jnp.exp(jnp.where(odd, 0.0, nxt - b[h])))
                al.append(_dot_nt(qh.astype(BF16), kh.astype(BF16)))
        a = []
        for h in heads:
            ah = jnp.zeros((c, c), F32)
            for i, s in enumerate(levels):
                ah = ah + jnp.where(ri // (2 * s) == ci // (2 * s), al[h * len(levels) + i], 0.0)
            k3 = k[h].reshape(c // 8, 8, dk)
            b3 = b[h].reshape(c // 8, 8, dk)
            for jj in range(8):
                kj = jnp.broadcast_to(k3[:, jj:jj + 1, :], k3.shape).reshape(c, dk)
                bj = jnp.broadcast_to(b3[:, jj:jj + 1, :], b3.shape).reshape(c, dk)
                keep = row % 8 >= jj
                tt = q[h] * kj * jnp.exp(jnp.where(keep, b[h] - bj, 0.0))
                colv = jnp.sum(tt, axis=-1, keepdims=True)
                hit = (ci == 8 * (ri // 8) + jj) & (ri % 8 >= jj)
                ah = ah + jnp.where(hit, colv, 0.0)
            a.append(ah.astype(BF16))
        o = [o[h] + _dot(a[h], v[h]) for h in heads]
        for h in heads:
            on = _rmsnorm(o[h], ng_ref[...], GLA_NORM_EPS)
            go = go_ref[rows, h * dv:(h + 1) * dv].astype(F32)
            o_ref[rows, h * dv:(h + 1) * dv] = (on * (go * _sigmoid(go))).astype(o_ref.dtype)
        return carry

    lax.fori_loop(0, n_chunks, chunk, 0)


def _gla_core(y, bcum, norm_g, bsz, seq):
    nh = GLA_HEADS
    dqk = bcum.shape[1]
    dvv = (y.shape[1] - 2 * dqk) // 2
    ts = GLA_SEQ_BLOCK
    ns = seq // ts
    rowblk = lambda b, s: b * ns + s
    return pl.pallas_call(
        functools.partial(_gla_body, n_chunks=ts // GLA_CHUNK, n_heads=nh),
        grid=(bsz, ns),
        in_specs=[
            pl.BlockSpec((ts, dqk), lambda b, s: (rowblk(b, s), 0)),
            pl.BlockSpec((ts, dqk), lambda b, s: (rowblk(b, s), 1)),
            pl.BlockSpec((ts, dvv), lambda b, s: (rowblk(b, s), (2 * dqk) // dvv)),
            pl.BlockSpec((ts, dvv), lambda b, s: (rowblk(b, s), (2 * dqk) // dvv + 1)),
            pl.BlockSpec((ts, dqk), lambda b, s: (rowblk(b, s), 0)),
            _full(norm_g.shape),
        ],
        out_specs=pl.BlockSpec((ts, dvv), lambda b, s: (rowblk(b, s), 0)),
        out_shape=jax.ShapeDtypeStruct((bsz * seq, dvv), BF16),
        scratch_shapes=[pltpu.VMEM((nh, dvv // nh, dqk // nh), F32)],
        compiler_params=_params(("parallel", "arbitrary")),
        name="gla_core",
    )(y, y, y, y, bcum, norm_g)


def _gla_layer(h, g6, w_in, w_gate2, b_gate, norm_g, w_o, bsz, seq):
    d = h.shape[1]
    dk_all = w_gate2.shape[1]
    rank = w_gate2.shape[0]
    n_main = w_in.shape[1] - rank
    w_main = w_in[:, :n_main].astype(BF16)
    w_low = jnp.pad(w_in[:, n_main:], ((0, 0), (0, LANES - rank))).astype(BF16)
    w2 = jnp.pad(w_gate2, ((0, LANES - rank), (0, 0))).astype(BF16)
    y = _norm_proj(h, g6, 2, w_main, jnp.zeros((1, n_main), F32), BF16)
    bcum = _gla_gate(h, g6, w_low, w2, b_gate[None, :])
    o = _gla_core(y, bcum, norm_g[None, :], bsz, seq)
    return _proj_res(o, w_o.astype(BF16), jnp.zeros((1, d), F32), h, g6, 3)


def _head_sum(x, gmat):
    hi, mid, lo = _split3(x)
    return _dot(hi, gmat) + _dot(mid, gmat) + _dot(lo, gmat)


def _rwkv_pre_body(h_ref, hp_ref, g_ref, mu_ref, wrkv_ref, w1_ref, w2_ref, a1_ref, a2_ref,
                   g1_ref, g2_ref, vec_ref, r_ref, k_ref, v_ref, lw_ref, na_ref, nb_ref, gt_ref,
                   *, blocks_per_seq):
    i = pl.program_id(0)
    tm, d = h_ref.shape
    gain = g_ref[2:3, :]
    x = _rmsnorm(h_ref[...], gain)
    prev = _rmsnorm(hp_ref[...], gain)[7:8, :]
    prev = jnp.where(i % blocks_per_seq == 0, 0.0, prev)
    row = lax.broadcasted_iota(jnp.int32, (tm, 1), 0)
    xs = jnp.where(row == 0, prev, pltpu.roll(x, shift=1, axis=0))
    xx = xs - x

    def mix(j):
        return (x + xx * mu_ref[j:j + 1, :]).astype(BF16)

    r = _dot(mix(0), wrkv_ref[0])
    k = _dot(mix(2), wrkv_ref[1])
    v = _dot(mix(3), wrkv_ref[2])
    zw = vec_ref[0:1, :] + _dot(jnp.tanh(_dot(mix(1), w1_ref[...])).astype(BF16), w2_ref[...])
    lw = -jnp.exp(-_softplus(-zw) - 0.5)
    a = _sigmoid(vec_ref[1:2, :] + _dot(_dot(mix(4), a1_ref[...]).astype(BF16), a2_ref[...]))
    gt = _dot(_sigmoid(_dot(mix(5), g1_ref[...])).astype(BF16), g2_ref[...])
    kk = k * vec_ref[2:3, :]
    li = lax.broadcasted_iota(jnp.int32, (LANES, LANES), 0) // RWKV_HEAD
    lj = lax.broadcasted_iota(jnp.int32, (LANES, LANES), 1) // RWKV_HEAD
    gmat = jnp.where(li == lj, 1.0, 0.0).astype(BF16)
    r_ref[...] = r.astype(r_ref.dtype)
    v_ref[...] = v.astype(v_ref.dtype)
    lw_ref[...] = lw
    gt_ref[...] = gt.astype(gt_ref.dtype)
    k_ref[...] = (k * (1.0 + (a - 1.0) * vec_ref[3:4, :])).astype(k_ref.dtype)
    for s in range(0, d, LANES):
        kks = kk[:, s:s + LANES]
        ss = _head_sum(kks * kks, gmat)
        kkn = kks / jnp.maximum(jnp.sqrt(ss), 1e-12)
        na_ref[:, s:s + LANES] = (-kkn).astype(na_ref.dtype)
        nb_ref[:, s:s + LANES] = (kkn * a[:, s:s + LANES]).astype(nb_ref.dtype)


def _rwkv_pre(h, g6, mu, wrkv, w1, w2, a1, a2, g1, g2, vec, seq):
    t, d = h.shape
    tm = TOKEN_BLOCK
    tok = pl.BlockSpec((tm, d), lambda i: (i, 0))
    ins = [h, h, g6, mu, wrkv, w1, w2, a1, a2, g1, g2, vec]
    in_specs = [tok, pl.BlockSpec((8, d), lambda i: (jnp.maximum(i * (tm // 8) - 1, 0), 0))]
    in_specs += [_full(x.shape) for x in ins[2:]]
    return pl.pallas_call(
        functools.partial(_rwkv_pre_body, blocks_per_seq=seq // tm),
        grid=(t // tm,),
        in_specs=in_specs,
        out_specs=[tok] * 7,
        out_shape=[jax.ShapeDtypeStruct((t, d), F32 if i == 3 else BF16) for i in range(7)],
        compiler_params=_params(("parallel",)),
        name="rwkv_pre",
    )(*ins)


def _rwkv_body(r_ref, k_ref, v_ref, lw_ref, na_ref, nb_ref, gt_ref, vec_ref, o_ref, s_ref,
               *, n_chunks, n_pairs):
    c = RWKV_CHUNK
    hd = RWKV_HEAD
    pairs = range(n_pairs)

    @pl.when(pl.program_id(2) == 0)
    def _():
        s_ref[...] = jnp.zeros_like(s_ref)

    lane = lax.broadcasted_iota(jnp.int32, (c, LANES), 1)
    head0 = lane < hd
    ri = lax.broadcasted_iota(jnp.int32, (2 * c, 2 * c), 0)
    ci = lax.broadcasted_iota(jnp.int32, (2 * c, 2 * c), 1)
    strict = ri > ci
    incl = ri >= ci
    tri = jnp.where(lax.broadcasted_iota(jnp.int32, (c, c), 0)
                    >= lax.broadcasted_iota(jnp.int32, (c, c), 1), 1.0, 0.0).astype(BF16)

    def stack(x):
        return jnp.concatenate([jnp.where(head0, x, 0.0), jnp.where(head0, 0.0, x)], axis=0)

    def hsum(x):
        s0 = jnp.sum(jnp.where(head0, x, 0.0), axis=-1, keepdims=True)
        s1 = jnp.sum(jnp.where(head0, 0.0, x), axis=-1, keepdims=True)
        return jnp.where(head0, s0, s1)

    def chunk(ic, carry):
        r0 = pl.multiple_of(ic * c, c)
        rows = pl.ds(r0, c)
        hi, mid, lo = _split3(lw_ref[rows, :])
        cl_all = _dot(tri, hi) + _dot(tri, mid) + _dot(tri, lo)
        lhs, rhs, rhs_end, vsb, sb, decay = [], [], [], [], [], []
        for p in pairs:
            sl = slice(p * LANES, (p + 1) * LANES)
            r, k, v = (x[rows, sl].astype(F32) for x in (r_ref, k_ref, v_ref))
            lw, na, nb = lw_ref[rows, sl], na_ref[rows, sl].astype(F32), nb_ref[rows, sl].astype(F32)
            cl = cl_all[:, sl]
            cl_last = cl[c - 1:c, :]
            p_inv = jnp.exp(-cl)
            p_end = jnp.exp(cl_last - cl)
            lhs.append(jnp.concatenate([stack(na * jnp.exp(cl - lw)), stack(r * jnp.exp(cl))],
                                       axis=0).astype(BF16))
            rhs.append(jnp.concatenate([stack(nb * p_inv), stack(k * p_inv)], axis=0).astype(BF16))
            rhs_end.append(jnp.concatenate([stack(nb * p_end), stack(k * p_end)], axis=0).astype(BF16))
            vsb.append(stack(v).astype(BF16))
            sb.append(s_ref[p].astype(BF16))
            decay.append(jnp.exp(cl_last))
        gram = [_dot_nt(lhs[p], rhs[p]) for p in pairs]
        ws = [_dot_nt(lhs[p], sb[p]) for p in pairs]
        a_ab = [jnp.where(strict, gram[p][:2 * c, :2 * c], 0.0) for p in pairs]
        a_ak = [jnp.where(strict, gram[p][:2 * c, 2 * c:], 0.0).astype(BF16) for p in pairs]
        a_r = [jnp.concatenate([jnp.where(incl, gram[p][2 * c:, :2 * c], 0.0),
                                jnp.where(incl, gram[p][2 * c:, 2 * c:], 0.0)], axis=1).astype(BF16)
               for p in pairs]
        u = [ws[p][:2 * c] + _dot(a_ak[p], vsb[p]) for p in pairs]
        apow = a_ab
        steps = (c - 1).bit_length()
        for it in range(steps):
            ab = [apow[p].astype(BF16) for p in pairs]
            u = [u[p] + _dot(ab[p], u[p].astype(BF16)) for p in pairs]
            if it + 1 < steps:
                apow = [_dot(ab[p], ab[p]) for p in pairs]
        uv = [jnp.concatenate([u[p].astype(BF16), vsb[p]], axis=0) for p in pairs]
        ys = [ws[p][2 * c:] + _dot(a_r[p], uv[p]) for p in pairs]
        for p in pairs:
            s_ref[p] = s_ref[p] * decay[p] + _dot_tn(uv[p], rhs_end[p])
        for p in pairs:
            sl = slice(p * LANES, (p + 1) * LANES)
            y = ys[p][:c, :] + ys[p][c:, :]
            mean = hsum(y) * (1.0 / hd)
            yc = y - mean
            var = hsum(yc * yc) * (1.0 / hd)
            yn = yc * lax.rsqrt(var + RWKV_LNX_EPS) * vec_ref[1:2, sl] + vec_ref[2:3, sl]
            r, k, v = (x[rows, sl].astype(F32) for x in (r_ref, k_ref, v_ref))
            yn = yn + hsum(r * k * vec_ref[0:1, sl]) * v
            o_ref[rows, sl] = (yn * gt_ref[rows, sl].astype(F32)).astype(o_ref.dtype)
        return carry

    lax.fori_loop(0, n_chunks, chunk, 0)


def _rwkv_core(r, k, v, lw, na, nb, gt, vec, bsz, seq):
    t, d = r.shape
    ts = RWKV_SEQ_BLOCK
    ns = seq // ts
    width = RWKV_PAIRS * LANES
    tok = pl.BlockSpec((ts, width), lambda b, p, s: (b * ns + s, p))
    return pl.pallas_call(
        functools.partial(_rwkv_body, n_chunks=ts // RWKV_CHUNK, n_pairs=RWKV_PAIRS),
        grid=(bsz, d // width, ns),
        in_specs=[tok] * 7 + [pl.BlockSpec((vec.shape[0], width), lambda b, p, s: (0, p))],
        out_specs=tok,
        out_shape=jax.ShapeDtypeStruct((t, d), BF16),
        scratch_shapes=[pltpu.VMEM((RWKV_PAIRS, LANES, LANES), F32)],
        compiler_params=_params(("parallel", "parallel", "arbitrary")),
        name="rwkv_core",
    )(r, k, v, lw, na, nb, gt, vec)


def _pad_to(x, axis, size):
    pads = [(0, 0)] * x.ndim
    pads[axis] = (0, size - x.shape[axis])
    return jnp.pad(x, pads)


def _rwkv_layer(h, g6, mu, w_rkv, w0, w1, w2, a0, a1, a2, g1, g2, k_k, k_a, r_k, lnx_g, lnx_b,
                w_o, bsz, seq):
    d = h.shape[1]
    up = lambda n: -(-n // LANES) * LANES
    w1p = _pad_to(w1, 1, up(w1.shape[1])).astype(BF16)
    w2p = _pad_to(w2, 0, up(w2.shape[0])).astype(BF16)
    a1p = _pad_to(a1, 1, up(a1.shape[1])).astype(BF16)
    a2p = _pad_to(a2, 0, up(a2.shape[0])).astype(BF16)
    g1p = _pad_to(g1, 1, up(g1.shape[1])).astype(BF16)
    g2p = _pad_to(g2, 0, up(g2.shape[0])).astype(BF16)
    vec_pre = _pad_to(jnp.stack([w0, a0, k_k, k_a]), 0, 8)
    outs = _rwkv_pre(h, g6, _pad_to(mu, 0, 8), w_rkv.astype(BF16), w1p, w2p, a1p, a2p, g1p, g2p,
                     vec_pre, seq)
    vec_core = _pad_to(jnp.stack([r_k.reshape(d), lnx_g, lnx_b]), 0, 8)
    o = _rwkv_core(*outs, vec_core, bsz, seq)
    return _proj_res(o, w_o.astype(BF16), jnp.zeros((1, d), F32), h, g6, 3)


def kernel(x, norm_g, ffn_w_in, ffn_w_out, swa_w_qkv, swa_b_qkv, swa_sinks, swa_w_o, swa_b_o, gla_w_in, gla_w_gate2, gla_b_gate, gla_norm_g, gla_w_o, rwkv_mu, rwkv_w_rkv, rwkv_w0, rwkv_w1, rwkv_w2, rwkv_a0, rwkv_a1, rwkv_a2, rwkv_g1, rwkv_g2, rwkv_k_k, rwkv_k_a, rwkv_r_k, rwkv_lnx_g, rwkv_lnx_b, rwkv_w_o):
    bsz, seq, d = x.shape
    depth = norm_g.shape[0]
    h = x.reshape(bsz * seq, d)
    for layer in range(depth):
        g6 = norm_g[layer]
        h = _ffn(h, g6, ffn_w_in[layer, 0], ffn_w_out[layer, 0], 0)
        kind, j = layer % 3, layer // 3
        if kind == 0:
            h = _swa_layer(h, g6, swa_w_qkv[j], swa_b_qkv[j], swa_sinks[j], swa_w_o[j], swa_b_o[j],
                           bsz, seq)
        elif kind == 1:
            h = _gla_layer(h, g6, gla_w_in[j], gla_w_gate2[j], gla_b_gate[j], gla_norm_g[j],
                           gla_w_o[j], bsz, seq)
        else:
            h = _rwkv_layer(h, g6, rwkv_mu[j], rwkv_w_rkv[j], rwkv_w0[j], rwkv_w1[j], rwkv_w2[j],
                            rwkv_a0[j], rwkv_a1[j], rwkv_a2[j], rwkv_g1[j], rwkv_g2[j],
                            rwkv_k_k[j], rwkv_k_a[j], rwkv_r_k[j], rwkv_lnx_g[j], rwkv_lnx_b[j],
                            rwkv_w_o[j], bsz, seq)
        h = _ffn(h, g6, ffn_w_in[layer, 1], ffn_w_out[layer, 1], 4)
    return h.reshape(bsz, seq, d)
```

```python
import functools

import jax
import jax.numpy as jnp
from jax import lax
from jax.experimental import pallas as pl
from jax.experimental.pallas import tpu as pltpu

F32 = jnp.float32
BF16 = jnp.bfloat16

NORM_EPS = 1e-6
GLA_NORM_EPS = 1e-5
GLA_GATE_TEMP = 16.0
RWKV_LNX_EPS = 64e-5

D_FF_CHUNK = 256
TOKEN_BLOCK = 512
FFN_TOKEN_BLOCK = 1024
FFN_SUB_ROWS = 512
SWA_BLOCK = 128
SWA_HEAD_DIM = 64
SWA_HEADS = 16
SWA_KV_HEADS = 4
GLA_HEADS = 4
GLA_CHUNK = 64
GLA_SEQ_BLOCK = 512
RWKV_HEAD = 64
RWKV_CHUNK = 64
RWKV_SEQ_BLOCK = 512
RWKV_PRE_SUB_ROWS = 512
RWKV_PAIRS = 8
LANES = 128
VMEM_LIMIT = 56 * 1024 * 1024


def _params(sem):
    return pltpu.CompilerParams(dimension_semantics=sem, vmem_limit_bytes=VMEM_LIMIT)


def _rmsnorm(x, g, eps=NORM_EPS):
    return x * lax.rsqrt(jnp.mean(x * x, axis=-1, keepdims=True) + eps) * g


def _sigmoid(x):
    return 1.0 / (1.0 + jnp.exp(-x))


def _softplus(x):
    return jnp.maximum(x, 0.0) + jnp.log(1.0 + jnp.exp(-jnp.abs(x)))


def _dot(a, b):
    return jnp.dot(a, b, preferred_element_type=F32)


def _dot_nt(a, b):
    return lax.dot_general(a, b, (((1,), (1,)), ((), ())), preferred_element_type=F32)


def _dot_tn(a, b):
    return lax.dot_general(a, b, (((0,), (0,)), ((), ())), preferred_element_type=F32)


def _full(shape):
    n = len(shape)
    return pl.BlockSpec(shape, lambda *_: (0,) * n)


def _ffn_body(*refs, g_row, mixer_proj):
    if mixer_proj:
        h_ref, a_ref, wo_ref, bo_ref, g_ref, win_ref, wout_ref, o_ref, xn_ref, acc_ref = refs
    else:
        h_ref, g_ref, win_ref, wout_ref, o_ref, xn_ref, acc_ref = refs
    d_ff = wout_ref.shape[0]
    tf = D_FF_CHUNK
    tm = h_ref.shape[0]
    subs = [slice(r0, r0 + FFN_SUB_ROWS) for r0 in range(0, tm, FFN_SUB_ROWS)]
    for rs in subs:
        hs = h_ref[rs, :]
        if mixer_proj:
            m = _dot(a_ref[rs, :], wo_ref[...]) + bo_ref[...]
            hs = hs + _rmsnorm(m, g_ref[g_row - 1:g_row, :])
            o_ref[rs, :] = hs
        xn_ref[rs, :] = _rmsnorm(hs, g_ref[g_row:g_row + 1, :]).astype(BF16)
    res_ref = o_ref if mixer_proj else h_ref
    for rs in subs:
        for c0 in range(0, d_ff, tf):
            xn = xn_ref[rs, :]
            gate = _dot(xn, win_ref[:, c0:c0 + tf])
            up = _dot(xn, win_ref[:, d_ff + c0:d_ff + c0 + tf])
            act = (gate * _sigmoid(gate) * up).astype(BF16)
            part = _dot(act, wout_ref[c0:c0 + tf, :])
            if c0 == 0:
                acc_ref[rs, :] = part
            else:
                acc_ref[rs, :] += part
    for rs in subs:
        o_ref[rs, :] = res_ref[rs, :] + 0.5 * _rmsnorm(acc_ref[rs, :], g_ref[g_row + 1:g_row + 2, :])


def _resident(shape):
    n = len(shape)
    return pl.BlockSpec(shape, lambda *_: (0,) * n, pipeline_mode=pl.Buffered(1))


def _stacked_weight(w, *lead):
    idx = tuple(lead) + (0, 0)
    return pl.BlockSpec((None,) * len(lead) + w.shape[-2:], lambda *_: idx,
                        pipeline_mode=pl.Buffered(1))


def _ffn(h, g6, win, wout, layer, which, mixer=None):
    t, d = h.shape
    g_row = 4 * which
    tm = FFN_TOKEN_BLOCK
    tok = pl.BlockSpec((tm, d), lambda i: (i, 0))
    ins, in_specs = [h], [tok]
    if mixer is not None:
        a, w_o, b_o = mixer
        ins += [a, w_o, b_o]
        in_specs += [pl.BlockSpec((tm, a.shape[1]), lambda i: (i, 0)), _resident(w_o.shape),
                     _full(b_o.shape)]
    ins += [g6, win, wout]
    in_specs += [_full(g6.shape), _stacked_weight(win, layer, which),
                 _stacked_weight(wout, layer, which)]
    return pl.pallas_call(
        functools.partial(_ffn_body, g_row=g_row, mixer_proj=mixer is not None),
        grid=(t // tm,),
        in_specs=in_specs,
        out_specs=tok,
        out_shape=jax.ShapeDtypeStruct((t, d), F32),
        scratch_shapes=[pltpu.VMEM((tm, d), BF16), pltpu.VMEM((tm, d), F32)],
        compiler_params=_params(("parallel",)),
        name="ffn_mix" if mixer is not None else "ffn",
    )(*ins)


def _norm_proj_body(h_ref, g_ref, w_ref, b_ref, o_ref, *, g_row, n_chunk):
    xn = _rmsnorm(h_ref[...], g_ref[g_row:g_row + 1, :]).astype(BF16)
    n = w_ref.shape[1]
    for c in range(0, n, n_chunk):
        y = _dot(xn, w_ref[:, c:c + n_chunk]) + b_ref[:, c:c + n_chunk]
        o_ref[:, c:c + n_chunk] = y.astype(o_ref.dtype)


def _norm_proj(h, g6, g_row, w, b, out_dtype):
    t, d = h.shape
    n = w.shape[1]
    tm = TOKEN_BLOCK
    return pl.pallas_call(
        functools.partial(_norm_proj_body, g_row=g_row, n_chunk=512),
        grid=(t // tm,),
        in_specs=[
            pl.BlockSpec((tm, d), lambda i: (i, 0)),
            _full(g6.shape),
            _full(w.shape),
            _full(b.shape),
        ],
        out_specs=pl.BlockSpec((tm, n), lambda i: (i, 0)),
        out_shape=jax.ShapeDtypeStruct((t, n), out_dtype),
        compiler_params=_params(("parallel",)),
        name="norm_proj",
    )(h, g6, w, b)


def _swa_body(sink_ref, q_ref, kp_ref, kc_ref, vp_ref, vc_ref, o_ref):
    n = pl.program_id(1)
    blk = SWA_BLOCK
    nkv = SWA_KV_HEADS
    kcat = jnp.concatenate([kp_ref[...], kc_ref[...]], axis=0)
    vcat = jnp.concatenate([vp_ref[...], vc_ref[...]], axis=0)
    q_idx = lax.broadcasted_iota(jnp.int32, (blk, 2 * blk), 0) + blk
    k_idx = lax.broadcasted_iota(jnp.int32, (blk, 2 * blk), 1)
    dist = q_idx - k_idx
    valid = (dist >= 0) & (dist < blk) & ((k_idx >= blk) | (n > 0))
    width = nkv * SWA_HEAD_DIM
    lane_head = lax.broadcasted_iota(jnp.int32, (2 * blk, width), 1) // SWA_HEAD_DIM
    scale = SWA_HEAD_DIM ** -0.5
    groups = SWA_HEADS // nkv
    kstack = jnp.concatenate(
        [kcat * jnp.where(lane_head == hh, scale, 0.0).astype(BF16) for hh in range(nkv)], axis=0)
    vstack = jnp.concatenate(
        [vcat * jnp.where(lane_head == hh, 1.0, 0.0).astype(BF16) for hh in range(nkv)], axis=0)
    def score(s):
        return _dot_nt(q_ref[:, s * width:(s + 1) * width], kstack)

    def softmax(s, sc4):
        ps = []
        for hh in range(nkv):
            sc = jnp.where(valid, sc4[:, hh * 2 * blk:(hh + 1) * 2 * blk], -jnp.inf)
            sink = sink_ref[hh * groups + s]
            m = jnp.maximum(jnp.max(sc, axis=-1, keepdims=True), sink)
            p = jnp.exp(sc - m)
            inv = 1.0 / (jnp.sum(p, axis=-1, keepdims=True) + jnp.exp(sink - m))
            ps.append((p * inv).astype(BF16))
        return jnp.concatenate(ps, axis=1)

    scores = [score(s) for s in range(groups)]
    probs = [softmax(s, scores[s]) for s in range(groups)]
    for s in range(groups):
        o_ref[:, s * width:(s + 1) * width] = _dot(probs[s], vstack).astype(o_ref.dtype)


def _swa_core(qkv, sinks, bsz, seq):
    nb = seq // SWA_BLOCK
    blk = SWA_BLOCK
    dq = SWA_HEADS * SWA_HEAD_DIM
    dkv = SWA_KV_HEADS * SWA_HEAD_DIM
    kcol = dq // dkv
    cur = lambda b, n: b * nb + n
    prev = lambda b, n: b * nb + jnp.maximum(n - 1, 0)
    return pl.pallas_call(
        _swa_body,
        grid=(bsz, nb),
        in_specs=[
            pl.BlockSpec(memory_space=pltpu.SMEM),
            pl.BlockSpec((blk, dq), lambda b, n: (cur(b, n), 0)),
            pl.BlockSpec((blk, dkv), lambda b, n: (prev(b, n), kcol)),
            pl.BlockSpec((blk, dkv), lambda b, n: (cur(b, n), kcol)),
            pl.BlockSpec((blk, dkv), lambda b, n: (prev(b, n), kcol + 1)),
            pl.BlockSpec((blk, dkv), lambda b, n: (cur(b, n), kcol + 1)),
        ],
        out_specs=pl.BlockSpec((blk, dq), lambda b, n: (cur(b, n), 0)),
        out_shape=jax.ShapeDtypeStruct((bsz * seq, dq), BF16),
        compiler_params=_params(("parallel", "parallel")),
        name="swa_core",
    )(sinks, qkv, qkv, qkv, qkv, qkv)


def _swa_layer(h, g6, w_qkv, b_qkv, sinks, w_o, b_o, bsz, seq):
    d = h.shape[1]
    hd, nh, nkv = SWA_HEAD_DIM, SWA_HEADS, SWA_KV_HEADS
    grp = nh // nkv
    dq = nh * hd
    wq = w_qkv[:, :dq].reshape(d, nkv, grp, hd).transpose(0, 2, 1, 3).reshape(d, dq)
    bq = b_qkv[:dq].reshape(nkv, grp, hd).transpose(1, 0, 2).reshape(dq)
    w = jnp.concatenate([wq, w_qkv[:, dq:]], axis=1).astype(BF16)
    b = jnp.concatenate([bq, b_qkv[dq:]])[None, :]
    wo = w_o.reshape(nkv, grp, hd, d).transpose(1, 0, 2, 3).reshape(dq, d).astype(BF16)
    qkv = _norm_proj(h, g6, 2, w, b, BF16)
    att = _swa_core(qkv, sinks, bsz, seq)
    return att, wo, b_o[None, :]


def _split3(x):
    hi = x.astype(BF16)
    r1 = x - hi.astype(F32)
    mid = r1.astype(BF16)
    lo = (r1 - mid.astype(F32)).astype(BF16)
    return hi, mid, lo


def _gla_gate_body(h_ref, g_ref, wl_ref, w2_ref, bg_ref, o_ref):
    tm = h_ref.shape[0]
    xn = _rmsnorm(h_ref[...], g_ref[2:3, :]).astype(BF16)
    g_low = _dot(xn, wl_ref[...]).astype(BF16)
    z = _dot(g_low, w2_ref[...]) + bg_ref[...]
    la = -_softplus(-z) * (1.0 / GLA_GATE_TEMP)
    row = lax.broadcasted_iota(jnp.int32, (tm, tm), 0)
    col = lax.broadcasted_iota(jnp.int32, (tm, tm), 1)
    tri = jnp.where((row >= col) & (row // GLA_CHUNK == col // GLA_CHUNK), 1.0, 0.0).astype(BF16)
    hi, mid, lo = _split3(la)
    o_ref[...] = _dot(tri, hi) + _dot(tri, mid) + _dot(tri, lo)


def _gla_gate(h, g6, w_low, w_gate2, b_gate):
    t, d = h.shape
    dk = w_gate2.shape[1]
    tm = TOKEN_BLOCK
    return pl.pallas_call(
        _gla_gate_body,
        grid=(t // tm,),
        in_specs=[
            pl.BlockSpec((tm, d), lambda i: (i, 0)),
            _full(g6.shape),
            _full(w_low.shape),
            _full(w_gate2.shape),
            _full(b_gate.shape),
        ],
        out_specs=pl.BlockSpec((tm, dk), lambda i: (i, 0)),
        out_shape=jax.ShapeDtypeStruct((t, dk), F32),
        compiler_params=_params(("parallel",)),
        name="gla_gate",
    )(h, g6, w_low, w_gate2, b_gate)


def _block_first(x, s):
    c, l = x.shape
    xb = x.reshape(c // s, s, l)
    return jnp.broadcast_to(xb[:, 0:1, :], (c // s, s, l)).reshape(c, l)


def _gla_body(q_ref, k_ref, v_ref, go_ref, b_ref, ng_ref, o_ref, st_ref, *, n_chunks, n_heads):
    c = GLA_CHUNK
    dk = q_ref.shape[1] // n_heads
    dv = v_ref.shape[1] // n_heads
    heads = range(n_heads)

    @pl.when(pl.program_id(1) == 0)
    def _():
        st_ref[...] = jnp.zeros_like(st_ref)

    row = lax.broadcasted_iota(jnp.int32, (c, 1), 0)
    ri = lax.broadcasted_iota(jnp.int32, (c, c), 0)
    ci = lax.broadcasted_iota(jnp.int32, (c, c), 1)
    levels = (8, 16, 32)

    def chunk(ic, carry):
        rows = pl.ds(pl.multiple_of(ic * c, c), c)
        q, k, v, b = [], [], [], []
        for h in heads:
            q.append(q_ref[rows, h * dk:(h + 1) * dk].astype(F32) * (dk ** -0.5))
            k.append(k_ref[rows, h * dk:(h + 1) * dk].astype(F32))
            v.append(v_ref[rows, h * dv:(h + 1) * dv])
            b.append(b_ref[rows, h * dk:(h + 1) * dk])
        st = [st_ref[h] for h in heads]
        o = [_dot_nt((q[h] * jnp.exp(b[h])).astype(BF16), st[h].astype(BF16)) for h in heads]
        for h in heads:
            b_last = b[h][c - 1:c, :]
            kd = (k[h] * jnp.exp(b_last - b[h])).astype(BF16)
            st_ref[h] = st[h] * jnp.exp(b_last) + _dot_tn(v[h], kd)
        al = []
        for h in heads:
            for s in levels:
                first = _block_first(b[h], s)
                nxt = jnp.concatenate([first[s:], first[:s]], axis=0)
                odd = (row // s) % 2 == 1
                qh = jnp.where(odd, q[h] * jnp.exp(jnp.where(odd, b[h] - first, 0.0)), 0.0)
                kh = jnp.where(odd, 0.0, k[h] * jnp.exp(jnp.where(odd, 0.0, nxt - b[h])))
                al.append(_dot_nt(qh.astype(BF16), kh.astype(BF16)))
        a = []
        for h in heads:
            ah = jnp.zeros((c, c), F32)
            for i, s in enumerate(levels):
                ah = ah + jnp.where(ri // (2 * s) == ci // (2 * s), al[h * len(levels) + i], 0.0)
            k3 = k[h].reshape(c // 8, 8, dk)
            b3 = b[h].reshape(c // 8, 8, dk)
            for jj in range(8):
                kj = jnp.broadcast_to(k3[:, jj:jj + 1, :], k3.shape).reshape(c, dk)
                bj = jnp.broadcast_to(b3[:, jj:jj + 1, :], b3.shape).reshape(c, dk)
                keep = row % 8 >= jj
                tt = q[h] * kj * jnp.exp(jnp.where(keep, b[h] - bj, 0.0))
                colv = jnp.sum(tt, axis=-1, keepdims=True)
                hit = (ci == 8 * (ri // 8) + jj) & (ri % 8 >= jj)
                ah = ah + jnp.where(hit, colv, 0.0)
            a.append(ah.astype(BF16))
        o = [o[h] + _dot(a[h], v[h]) for h in heads]
        for h in heads:
            on = _rmsnorm(o[h], ng_ref[...], GLA_NORM_EPS)
            go = go_ref[rows, h * dv:(h + 1) * dv].astype(F32)
            o_ref[rows, h * dv:(h + 1) * dv] = (on * (go * _sigmoid(go))).astype(o_ref.dtype)
        return carry

    lax.fori_loop(0, n_chunks, chunk, 0)


def _gla_core(y, bcum, norm_g, bsz, seq):
    nh = GLA_HEADS
    dqk = bcum.shape[1]
    dvv = (y.shape[1] - 2 * dqk) // 2
    ts = GLA_SEQ_BLOCK
    ns = seq // ts
    rowblk = lambda b, s: b * ns + s
    return pl.pallas_call(
        functools.partial(_gla_body, n_chunks=ts // GLA_CHUNK, n_heads=nh),
        grid=(bsz, ns),
        in_specs=[
            pl.BlockSpec((ts, dqk), lambda b, s: (rowblk(b, s), 0)),
            pl.BlockSpec((ts, dqk), lambda b, s: (rowblk(b, s), 1)),
            pl.BlockSpec((ts, dvv), lambda b, s: (rowblk(b, s), (2 * dqk) // dvv)),
            pl.BlockSpec((ts, dvv), lambda b, s: (rowblk(b, s), (2 * dqk) // dvv + 1)),
            pl.BlockSpec((ts, dqk), lambda b, s: (rowblk(b, s), 0)),
            _full(norm_g.shape),
        ],
        out_specs=pl.BlockSpec((ts, dvv), lambda b, s: (rowblk(b, s), 0)),
        out_shape=jax.ShapeDtypeStruct((bsz * seq, dvv), BF16),
        scratch_shapes=[pltpu.VMEM((nh, dvv // nh, dqk // nh), F32)],
        compiler_params=_params(("parallel", "arbitrary")),
        name="gla_core",
    )(y, y, y, y, bcum, norm_g)


def _gla_layer(h, g6, w_in, w_gate2, b_gate, norm_g, w_o, bsz, seq):
    d = h.shape[1]
    dk_all = w_gate2.shape[1]
    rank = w_gate2.shape[0]
    n_main = w_in.shape[1] - rank
    w_main = w_in[:, :n_main].astype(BF16)
    w_low = jnp.pad(w_in[:, n_main:], ((0, 0), (0, LANES - rank))).astype(BF16)
    w2 = jnp.pad(w_gate2, ((0, LANES - rank), (0, 0))).astype(BF16)
    y = _norm_proj(h, g6, 2, w_main, jnp.zeros((1, n_main), F32), BF16)
    bcum = _gla_gate(h, g6, w_low, w2, b_gate[None, :])
    o = _gla_core(y, bcum, norm_g[None, :], bsz, seq)
    return o, w_o.astype(BF16), jnp.zeros((1, d), F32)


def _rwkv_pre_body(h_ref, hp_ref, g_ref, mu_ref, wrkv_ref, w1_ref, w2_ref, a1_ref, a2_ref,
                   g1_ref, g2_ref, vec_ref, r_ref, k_ref, v_ref, lw_ref, na_ref, nb_ref, gt_ref,
                   *, blocks_per_seq):
    i = pl.program_id(0)
    tm, d = h_ref.shape
    gain = g_ref[2:3, :]
    x = _rmsnorm(h_ref[...], gain)
    prev = _rmsnorm(hp_ref[...], gain)[7:8, :]
    prev = jnp.where(i % blocks_per_seq == 0, 0.0, prev)
    row = lax.broadcasted_iota(jnp.int32, (tm, 1), 0)
    xs = jnp.where(row == 0, prev, pltpu.roll(x, shift=1, axis=0))
    xx_all = xs - x
    li = lax.broadcasted_iota(jnp.int32, (LANES, LANES), 0) // RWKV_HEAD
    lj = lax.broadcasted_iota(jnp.int32, (LANES, LANES), 1) // RWKV_HEAD
    gmat = jnp.where(li == lj, 1.0, 0.0).astype(BF16)
    for r0 in range(0, tm, RWKV_PRE_SUB_ROWS):
        rs = slice(r0, r0 + RWKV_PRE_SUB_ROWS)
        xr, xx = x[rs, :], xx_all[rs, :]

        def mix(j):
            return (xr + xx * mu_ref[j:j + 1, :]).astype(BF16)

        r = _dot(mix(0), wrkv_ref[0])
        k = _dot(mix(2), wrkv_ref[1])
        v = _dot(mix(3), wrkv_ref[2])
        zw = vec_ref[0:1, :] + _dot(jnp.tanh(_dot(mix(1), w1_ref[...])).astype(BF16), w2_ref[...])
        lw = -jnp.exp(-_softplus(-zw) - 0.5)
        a = _sigmoid(vec_ref[1:2, :] + _dot(_dot(mix(4), a1_ref[...]).astype(BF16), a2_ref[...]))
        gt = _dot(_sigmoid(_dot(mix(5), g1_ref[...])).astype(BF16), g2_ref[...])
        kk = k * vec_ref[2:3, :]
        r_ref[rs, :] = r.astype(r_ref.dtype)
        v_ref[rs, :] = v.astype(v_ref.dtype)
        lw_ref[rs, :] = lw
        gt_ref[rs, :] = gt.astype(gt_ref.dtype)
        k_ref[rs, :] = (k * (1.0 + (a - 1.0) * vec_ref[3:4, :])).astype(k_ref.dtype)
        for s in range(0, d, LANES):
            kks = kk[:, s:s + LANES]
            ss = _dot((kks * kks).astype(BF16), gmat)
            kkn = kks / jnp.maximum(jnp.sqrt(ss), 1e-12)
            na_ref[rs, s:s + LANES] = (-kkn).astype(na_ref.dtype)
            nb_ref[rs, s:s + LANES] = (kkn * a[:, s:s + LANES]).astype(nb_ref.dtype)


def _rwkv_pre(h, g6, mu, wrkv, w1, w2, a1, a2, g1, g2, vec, seq):
    t, d = h.shape
    tm = TOKEN_BLOCK
    tok = pl.BlockSpec((tm, d), lambda i: (i, 0))
    ins = [h, h, g6, mu, wrkv, w1, w2, a1, a2, g1, g2, vec]
    in_specs = [tok, pl.BlockSpec((8, d), lambda i: (jnp.maximum(i * (tm // 8) - 1, 0), 0))]
    in_specs += [_full(x.shape) for x in ins[2:]]
    return pl.pallas_call(
        functools.partial(_rwkv_pre_body, blocks_per_seq=seq // tm),
        grid=(t // tm,),
        in_specs=in_specs,
        out_specs=[tok] * 7,
        out_shape=[jax.ShapeDtypeStruct((t, d), F32 if i == 3 else BF16) for i in range(7)],
        compiler_params=_params(("parallel",)),
        name="rwkv_pre",
    )(*ins)


def _rwkv_body(r_ref, k_ref, v_ref, lw_ref, na_ref, nb_ref, gt_ref, vec_ref, o_ref, s_ref,
               *, n_chunks, n_pairs):
    c = RWKV_CHUNK
    hd = RWKV_HEAD
    pairs = range(n_pairs)

    @pl.when(pl.program_id(2) == 0)
    def _():
        s_ref[...] = jnp.zeros_like(s_ref)

    lane = lax.broadcasted_iota(jnp.int32, (c, LANES), 1)
    head0 = lane < hd
    ri = lax.broadcasted_iota(jnp.int32, (2 * c, 2 * c), 0)
    ci = lax.broadcasted_iota(jnp.int32, (2 * c, 2 * c), 1)
    strict = ri > ci
    incl = ri >= ci
    tri = jnp.where(lax.broadcasted_iota(jnp.int32, (c, c), 0)
                    >= lax.broadcasted_iota(jnp.int32, (c, c), 1), 1.0, 0.0).astype(BF16)

    def stack(x):
        return jnp.concatenate([jnp.where(head0, x, 0.0), jnp.where(head0, 0.0, x)], axis=0)

    def hsum(x):
        s0 = jnp.sum(jnp.where(head0, x, 0.0), axis=-1, keepdims=True)
        s1 = jnp.sum(jnp.where(head0, 0.0, x), axis=-1, keepdims=True)
        return jnp.where(head0, s0, s1)

    def chunk(ic, carry):
        r0 = pl.multiple_of(ic * c, c)
        rows = pl.ds(r0, c)
        hi, mid, lo = _split3(lw_ref[rows, :])
        cl_all = _dot(tri, hi) + _dot(tri, mid) + _dot(tri, lo)
        lhs, rhs, rhs_end, vsb, sb, decay = [], [], [], [], [], []
        for p in pairs:
            sl = slice(p * LANES, (p + 1) * LANES)
            r, k, v = (x[rows, sl].astype(F32) for x in (r_ref, k_ref, v_ref))
            lw, na, nb = lw_ref[rows, sl], na_ref[rows, sl].astype(F32), nb_ref[rows, sl].astype(F32)
            cl = cl_all[:, sl]
            cl_last = cl[c - 1:c, :]
            p_inv = jnp.exp(-cl)
            p_end = jnp.exp(cl_last - cl)
            lhs.append(jnp.concatenate([stack(na * jnp.exp(cl - lw)), stack(r * jnp.exp(cl))],
                                       axis=0).astype(BF16))
            rhs.append(jnp.concatenate([stack(nb * p_inv), stack(k * p_inv)], axis=0).astype(BF16))
            rhs_end.append(jnp.concatenate([stack(nb * p_end), stack(k * p_end)], axis=0).astype(BF16))
            vsb.append(stack(v).astype(BF16))
            sb.append(s_ref[p].astype(BF16))
            decay.append(jnp.exp(cl_last))
        gram = [_dot_nt(lhs[p], rhs[p]) for p in pairs]
        ws = [_dot_nt(lhs[p], sb[p]) for p in pairs]
        a_ab = [jnp.where(strict, gram[p][:2 * c, :2 * c], 0.0) for p in pairs]
        a_ak = [jnp.where(strict, gram[p][:2 * c, 2 * c:], 0.0).astype(BF16) for p in pairs]
        a_r = [jnp.concatenate([jnp.where(incl, gram[p][2 * c:, :2 * c], 0.0),
                                jnp.where(incl, gram[p][2 * c:, 2 * c:], 0.0)], axis=1).astype(BF16)
               for p in pairs]
        u = [ws[p][:2 * c] + _dot(a_ak[p], vsb[p]) for p in pairs]
        apow = a_ab
        steps = (c - 1).bit_length()
        for it in range(steps):
            ab = [apow[p].astype(BF16) for p in pairs]
            if it + 1 < steps:
                both = [_dot(ab[p], jnp.concatenate([u[p].astype(BF16), ab[p]], axis=1))
                        for p in pairs]
                u = [u[p] + both[p][:, :LANES] for p in pairs]
                apow = [both[p][:, LANES:] for p in pairs]
            else:
                u = [u[p] + _dot(ab[p], u[p].astype(BF16)) for p in pairs]
        uv = [jnp.concatenate([u[p].astype(BF16), vsb[p]], axis=0) for p in pairs]
        ys = [ws[p][2 * c:] + _dot(a_r[p], uv[p]) for p in pairs]
        for p in pairs:
            s_ref[p] = s_ref[p] * decay[p] + _dot_tn(uv[p], rhs_end[p])
        for p in pairs:
            sl = slice(p * LANES, (p + 1) * LANES)
            y = ys[p][:c, :] + ys[p][c:, :]
            mean = hsum(y) * (1.0 / hd)
            yc = y - mean
            var = hsum(yc * yc) * (1.0 / hd)
            yn = yc * lax.rsqrt(var + RWKV_LNX_EPS) * vec_ref[1:2, sl] + vec_ref[2:3, sl]
            r, k, v = (x[rows, sl].astype(F32) for x in (r_ref, k_ref, v_ref))
            yn = yn + hsum(r * k * vec_ref[0:1, sl]) * v
            o_ref[rows, sl] = (yn * gt_ref[rows, sl].astype(F32)).astype(o_ref.dtype)
        return carry

    lax.fori_loop(0, n_chunks, chunk, 0)


def _rwkv_core(r, k, v, lw, na, nb, gt, vec, bsz, seq):
    t, d = r.shape
    ts = RWKV_SEQ_BLOCK
    ns = seq // ts
    width = RWKV_PAIRS * LANES
    tok = pl.BlockSpec((ts, width), lambda b, p, s: (b * ns + s, p))
    return pl.pallas_call(
        functools.partial(_rwkv_body, n_chunks=ts // RWKV_CHUNK, n_pairs=RWKV_PAIRS),
        grid=(bsz, d // width, ns),
        in_specs=[tok] * 7 + [pl.BlockSpec((vec.shape[0], width), lambda b, p, s: (0, p))],
        out_specs=tok,
        out_shape=jax.ShapeDtypeStruct((t, d), BF16),
        scratch_shapes=[pltpu.VMEM((RWKV_PAIRS, LANES, LANES), F32)],
        compiler_params=_params(("parallel", "parallel", "arbitrary")),
        name="rwkv_core",
    )(r, k, v, lw, na, nb, gt, vec)


def _pad_to(x, axis, size):
    pads = [(0, 0)] * x.ndim
    pads[axis] = (0, size - x.shape[axis])
    return jnp.pad(x, pads)


def _rwkv_layer(h, g6, mu, w_rkv, w0, w1, w2, a0, a1, a2, g1, g2, k_k, k_a, r_k, lnx_g, lnx_b,
                w_o, bsz, seq):
    d = h.shape[1]
    up = lambda n: -(-n // LANES) * LANES
    w1p = _pad_to(w1, 1, up(w1.shape[1])).astype(BF16)
    w2p = _pad_to(w2, 0, up(w2.shape[0])).astype(BF16)
    a1p = _pad_to(a1, 1, up(a1.shape[1])).astype(BF16)
    a2p = _pad_to(a2, 0, up(a2.shape[0])).astype(BF16)
    g1p = _pad_to(g1, 1, up(g1.shape[1])).astype(BF16)
    g2p = _pad_to(g2, 0, up(g2.shape[0])).astype(BF16)
    vec_pre = _pad_to(jnp.stack([w0, a0, k_k, k_a]), 0, 8)
    outs = _rwkv_pre(h, g6, _pad_to(mu, 0, 8), w_rkv.astype(BF16), w1p, w2p, a1p, a2p, g1p, g2p,
                     vec_pre, seq)
    vec_core = _pad_to(jnp.stack([r_k.reshape(d), lnx_g, lnx_b]), 0, 8)
    o = _rwkv_core(*outs, vec_core, bsz, seq)
    return o, w_o.astype(BF16), jnp.zeros((1, d), F32)


def kernel(x, norm_g, ffn_w_in, ffn_w_out, swa_w_qkv, swa_b_qkv, swa_sinks, swa_w_o, swa_b_o, gla_w_in, gla_w_gate2, gla_b_gate, gla_norm_g, gla_w_o, rwkv_mu, rwkv_w_rkv, rwkv_w0, rwkv_w1, rwkv_w2, rwkv_a0, rwkv_a1, rwkv_a2, rwkv_g1, rwkv_g2, rwkv_k_k, rwkv_k_a, rwkv_r_k, rwkv_lnx_g, rwkv_lnx_b, rwkv_w_o):
    bsz, seq, d = x.shape
    depth = norm_g.shape[0]
    h = x.reshape(bsz * seq, d)
    win, wout = ffn_w_in.astype(BF16), ffn_w_out.astype(BF16)
    for layer in range(depth):
        g6 = norm_g[layer]
        h = _ffn(h, g6, win, wout, layer, 0)
        kind, j = layer % 3, layer // 3
        if kind == 0:
            mix = _swa_layer(h, g6, swa_w_qkv[j], swa_b_qkv[j], swa_sinks[j], swa_w_o[j],
                             swa_b_o[j], bsz, seq)
        elif kind == 1:
            mix = _gla_layer(h, g6, gla_w_in[j], gla_w_gate2[j], gla_b_gate[j], gla_norm_g[j],
                             gla_w_o[j], bsz, seq)
        else:
            mix = _rwkv_layer(h, g6, rwkv_mu[j], rwkv_w_rkv[j], rwkv_w0[j], rwkv_w1[j], rwkv_w2[j],
                              rwkv_a0[j], rwkv_a1[j], rwkv_a2[j], rwkv_g1[j], rwkv_g2[j],
                              rwkv_k_k[j], rwkv_k_a[j], rwkv_r_k[j], rwkv_lnx_g[j], rwkv_lnx_b[j],
                              rwkv_w_o[j], bsz, seq)
        h = _ffn(h, g6, win, wout, layer, 1, mixer=mix)
    return h.reshape(bsz, seq, d)
```

```python
import functools

import jax
import jax.numpy as jnp
from jax import lax
from jax.experimental import pallas as pl
from jax.experimental.pallas import tpu as pltpu

F32 = jnp.float32
BF16 = jnp.bfloat16

NORM_EPS = 1e-6
GLA_NORM_EPS = 1e-5
GLA_GATE_TEMP = 16.0
RWKV_LNX_EPS = 64e-5
LOG2_E = 1.4426950408889634

D_FF_CHUNK = 256
TOKEN_BLOCK = 512
FFN_TOKEN_BLOCK = 1024
FFN_SUB_ROWS = 512
SWA_BLOCK = 128
SWA_Q_BLOCKS = 2
SWA_HEAD_DIM = 64
SWA_HEADS = 16
SWA_KV_HEADS = 4
GLA_HEADS = 4
GLA_CHUNK = 64
GLA_SEQ_BLOCK = 512
RWKV_HEAD = 64
RWKV_CHUNK = 64
RWKV_SEQ_BLOCK = 512
RWKV_PRE_SUB_ROWS = 512
RWKV_PAIRS = 8
LANES = 128
VMEM_LIMIT = 56 * 1024 * 1024


def _params(sem):
    return pltpu.CompilerParams(dimension_semantics=sem, vmem_limit_bytes=VMEM_LIMIT)


def _rmsnorm(x, g, eps=NORM_EPS):
    return x * lax.rsqrt(jnp.mean(x * x, axis=-1, keepdims=True) + eps) * g


def _sigmoid(x):
    return 1.0 / (1.0 + jnp.exp(-x))


def _softplus(x):
    return jnp.maximum(x, 0.0) + jnp.log(1.0 + jnp.exp(-jnp.abs(x)))


def _dot(a, b):
    return jnp.dot(a, b, preferred_element_type=F32)


def _dot_nt(a, b):
    return lax.dot_general(a, b, (((1,), (1,)), ((), ())), preferred_element_type=F32)


def _dot_tn(a, b):
    return lax.dot_general(a, b, (((0,), (0,)), ((), ())), preferred_element_type=F32)


def _full(shape):
    n = len(shape)
    return pl.BlockSpec(shape, lambda *_: (0,) * n)


def _ffn_body(*refs, g_row, mixer_proj):
    if mixer_proj:
        h_ref, a_ref, wo_ref, bo_ref, g_ref, win_ref, wout_ref, o_ref, xn_ref, acc_ref = refs
    else:
        h_ref, g_ref, win_ref, wout_ref, o_ref, xn_ref, acc_ref = refs
    d_ff = wout_ref.shape[0]
    tf = D_FF_CHUNK
    tm = h_ref.shape[0]
    subs = [slice(r0, r0 + FFN_SUB_ROWS) for r0 in range(0, tm, FFN_SUB_ROWS)]
    for rs in subs:
        hs = h_ref[rs, :]
        if mixer_proj:
            m = _dot(a_ref[rs, :], wo_ref[...]) + bo_ref[...]
            hs = hs + _rmsnorm(m, g_ref[g_row - 1:g_row, :])
            o_ref[rs, :] = hs
        xn_ref[rs, :] = _rmsnorm(hs, g_ref[g_row:g_row + 1, :]).astype(BF16)
    res_ref = o_ref if mixer_proj else h_ref
    for rs in subs:
        for c0 in range(0, d_ff, tf):
            xn = xn_ref[rs, :]
            gate = _dot(xn, win_ref[:, c0:c0 + tf])
            up = _dot(xn, win_ref[:, d_ff + c0:d_ff + c0 + tf])
            act = (gate * _sigmoid(gate) * up).astype(BF16)
            part = _dot(act, wout_ref[c0:c0 + tf, :])
            if c0 == 0:
                acc_ref[rs, :] = part
            else:
                acc_ref[rs, :] += part
    for rs in subs:
        o_ref[rs, :] = res_ref[rs, :] + 0.5 * _rmsnorm(acc_ref[rs, :], g_ref[g_row + 1:g_row + 2, :])


def _resident(shape):
    n = len(shape)
    return pl.BlockSpec(shape, lambda *_: (0,) * n, pipeline_mode=pl.Buffered(1))


def _stacked_weight(w, *lead):
    idx = tuple(lead) + (0, 0)
    return pl.BlockSpec((None,) * len(lead) + w.shape[-2:], lambda *_: idx,
                        pipeline_mode=pl.Buffered(1))


def _ffn(h, g6, win, wout, layer, which, mixer=None):
    t, d = h.shape
    g_row = 4 * which
    tm = FFN_TOKEN_BLOCK
    tok = pl.BlockSpec((tm, d), lambda i: (i, 0))
    ins, in_specs = [h], [tok]
    if mixer is not None:
        a, w_o, b_o = mixer
        ins += [a, w_o, b_o]
        in_specs += [pl.BlockSpec((tm, a.shape[1]), lambda i: (i, 0)), _resident(w_o.shape),
                     _full(b_o.shape)]
    ins += [g6, win, wout]
    in_specs += [_full(g6.shape), _stacked_weight(win, layer, which),
                 _stacked_weight(wout, layer, which)]
    return pl.pallas_call(
        functools.partial(_ffn_body, g_row=g_row, mixer_proj=mixer is not None),
        grid=(t // tm,),
        in_specs=in_specs,
        out_specs=tok,
        out_shape=jax.ShapeDtypeStruct((t, d), F32),
        scratch_shapes=[pltpu.VMEM((tm, d), BF16), pltpu.VMEM((tm, d), F32)],
        compiler_params=_params(("parallel",)),
        name="ffn_mix" if mixer is not None else "ffn",
    )(*ins)


def _norm_proj_body(h_ref, g_ref, w_ref, b_ref, *o_refs, g_row, n_chunk):
    xn = _rmsnorm(h_ref[...], g_ref[g_row:g_row + 1, :]).astype(BF16)
    col = 0
    for o_ref in o_refs:
        n = o_ref.shape[1]
        for c in range(0, n, n_chunk):
            cs = slice(col + c, col + min(c + n_chunk, n))
            y = _dot(xn, w_ref[:, cs]) + b_ref[:, cs]
            o_ref[:, c:c + cs.stop - cs.start] = y.astype(o_ref.dtype)
        col += n


def _norm_proj(h, g6, g_row, w, b, widths):
    t, d = h.shape
    tm = TOKEN_BLOCK
    return pl.pallas_call(
        functools.partial(_norm_proj_body, g_row=g_row, n_chunk=512),
        grid=(t // tm,),
        in_specs=[
            pl.BlockSpec((tm, d), lambda i: (i, 0)),
            _full(g6.shape),
            _resident(w.shape),
            _full(b.shape),
        ],
        out_specs=[pl.BlockSpec((tm, n), lambda i: (i, 0)) for n in widths],
        out_shape=[jax.ShapeDtypeStruct((t, n), BF16) for n in widths],
        compiler_params=_params(("parallel",)),
        name="norm_proj",
    )(h, g6, w, b)


def _swa_body(sink_ref, q_ref, kp_ref, kc_ref, vp_ref, vc_ref, o_ref):
    first_step = pl.program_id(1) == 0
    blk = SWA_BLOCK
    nkv = SWA_KV_HEADS
    nq = q_ref.shape[0] // blk
    kall = jnp.concatenate([kp_ref[...], kc_ref[...]], axis=0)
    vall = jnp.concatenate([vp_ref[...], vc_ref[...]], axis=0)
    own = (lax.broadcasted_iota(jnp.int32, (blk, blk), 1)
           <= lax.broadcasted_iota(jnp.int32, (blk, blk), 0))
    no_prev = jnp.where(first_step, -jnp.inf, 0.0)
    width = nkv * SWA_HEAD_DIM
    lane_head = lax.broadcasted_iota(jnp.int32, kall.shape, 1) // SWA_HEAD_DIM
    scale = SWA_HEAD_DIM ** -0.5
    groups = SWA_HEADS // nkv
    kmask = [kall * jnp.where(lane_head == hh, scale, 0.0).astype(BF16) for hh in range(nkv)]
    vmask = [vall * jnp.where(lane_head == hh, 1.0, 0.0).astype(BF16) for hh in range(nkv)]
    work = [(j, s) for j in range(nq) for s in range(groups)]

    def keys_of(masked, j):
        return jnp.concatenate([m[j * blk:(j + 2) * blk] for m in masked], axis=0)

    def softmax(j, s, sc4):
        ps = []
        for hh in range(nkv):
            c0 = hh * 2 * blk
            prev = sc4[:, c0:c0 + blk]
            if j == 0:
                prev = prev + no_prev
            sc = jnp.where(own, sc4[:, c0 + blk:c0 + 2 * blk], prev)
            sink = sink_ref[hh * groups + s]
            m = jnp.maximum(jnp.max(sc, axis=-1, keepdims=True), sink)
            p = jnp.exp(sc - m)
            inv = 1.0 / (jnp.sum(p, axis=-1, keepdims=True) + jnp.exp(sink - m))
            p = p * inv
            ps.append(jnp.where(own, 0.0, p).astype(BF16))
            ps.append(jnp.where(own, p, 0.0).astype(BF16))
        return jnp.concatenate(ps, axis=1)

    kstack = [keys_of(kmask, j) for j in range(nq)]
    vstack = [keys_of(vmask, j) for j in range(nq)]
    scores = [_dot_nt(q_ref[j * blk:(j + 1) * blk, s * width:(s + 1) * width], kstack[j])
              for j, s in work]
    probs = [softmax(j, s, scores[i]) for i, (j, s) in enumerate(work)]
    for i, (j, s) in enumerate(work):
        o_ref[j * blk:(j + 1) * blk, s * width:(s + 1) * width] = (
            _dot(probs[i], vstack[j]).astype(o_ref.dtype))


def _swa_core(q, k, v, sinks, bsz, seq):
    blk = SWA_BLOCK
    nq = SWA_Q_BLOCKS
    ns = seq // (nq * blk)
    dq, dkv = q.shape[1], k.shape[1]
    cur = lambda b, n: (b * ns + n, 0)
    prev = lambda b, n: ((b * ns + n) * nq - jnp.minimum(n, 1), 0)
    return pl.pallas_call(
        _swa_body,
        grid=(bsz, ns),
        in_specs=[
            pl.BlockSpec(memory_space=pltpu.SMEM),
            pl.BlockSpec((nq * blk, dq), cur),
            pl.BlockSpec((blk, dkv), prev),
            pl.BlockSpec((nq * blk, dkv), cur),
            pl.BlockSpec((blk, dkv), prev),
            pl.BlockSpec((nq * blk, dkv), cur),
        ],
        out_specs=pl.BlockSpec((nq * blk, dq), cur),
        out_shape=jax.ShapeDtypeStruct((bsz * seq, dq), BF16),
        compiler_params=_params(("parallel", "parallel")),
        name="swa_core",
    )(sinks, q, k, k, v, v)


def _swa_layer(h, g6, w_qkv, b_qkv, sinks, w_o, b_o, bsz, seq):
    d = h.shape[1]
    hd, nh, nkv = SWA_HEAD_DIM, SWA_HEADS, SWA_KV_HEADS
    grp = nh // nkv
    dq = nh * hd
    wq = w_qkv[:, :dq].reshape(d, nkv, grp, hd).transpose(0, 2, 1, 3).reshape(d, dq)
    bq = b_qkv[:dq].reshape(nkv, grp, hd).transpose(1, 0, 2).reshape(dq)
    w = jnp.concatenate([wq, w_qkv[:, dq:]], axis=1).astype(BF16)
    b = jnp.concatenate([bq, b_qkv[dq:]])[None, :]
    wo = w_o.reshape(nkv, grp, hd, d).transpose(1, 0, 2, 3).reshape(dq, d).astype(BF16)
    dkv = nkv * hd
    q, k, v = _norm_proj(h, g6, 2, w, b, (dq, dkv, dkv))
    att = _swa_core(q, k, v, sinks, bsz, seq)
    return att, wo, b_o[None, :]


def _split3(x):
    hi = x.astype(BF16)
    r1 = x - hi.astype(F32)
    mid = r1.astype(BF16)
    lo = (r1 - mid.astype(F32)).astype(BF16)
    return hi, mid, lo


def _gla_in_body(h_ref, g_ref, w_ref, wl_ref, w2_ref, bg_ref, y_ref, b_ref, *, n_chunk):
    tm = h_ref.shape[0]
    c = GLA_CHUNK
    xn = _rmsnorm(h_ref[...], g_ref[2:3, :]).astype(BF16)
    for c0 in range(0, w_ref.shape[1], n_chunk):
        y_ref[:, c0:c0 + n_chunk] = _dot(xn, w_ref[:, c0:c0 + n_chunk]).astype(y_ref.dtype)
    g_low = _dot(xn, wl_ref[...]).astype(BF16)
    z = _dot(g_low, w2_ref[...]) + bg_ref[...]
    la = -_softplus(-z) * (LOG2_E / GLA_GATE_TEMP)
    tri = jnp.where(lax.broadcasted_iota(jnp.int32, (c, c), 0)
                    >= lax.broadcasted_iota(jnp.int32, (c, c), 1), 1.0, 0.0).astype(BF16)
    for r0 in range(0, tm, c):
        hi, mid, lo = _split3(la[r0:r0 + c, :])
        b_ref[r0:r0 + c, :] = _dot(tri, hi) + _dot(tri, mid) + _dot(tri, lo)


def _gla_in(h, g6, w_main, w_low, w_gate2, b_gate):
    t, d = h.shape
    n = w_main.shape[1]
    dk = w_gate2.shape[1]
    tm = TOKEN_BLOCK
    return pl.pallas_call(
        functools.partial(_gla_in_body, n_chunk=512),
        grid=(t // tm,),
        in_specs=[
            pl.BlockSpec((tm, d), lambda i: (i, 0)),
            _full(g6.shape),
            _resident(w_main.shape),
            _resident(w_low.shape),
            _resident(w_gate2.shape),
            _full(b_gate.shape),
        ],
        out_specs=[pl.BlockSpec((tm, n), lambda i: (i, 0)), pl.BlockSpec((tm, dk), lambda i: (i, 0))],
        out_shape=[jax.ShapeDtypeStruct((t, n), BF16), jax.ShapeDtypeStruct((t, dk), F32)],
        compiler_params=_params(("parallel",)),
        name="gla_in",
    )(h, g6, w_main, w_low, w_gate2, b_gate)


def _block_first(x, s):
    c, l = x.shape
    xb = x.reshape(c // s, s, l)
    return jnp.broadcast_to(xb[:, 0:1, :], (c // s, s, l)).reshape(c, l)


def _gla_body(q_ref, k_ref, v_ref, go_ref, b_ref, ng_ref, o_ref, st_ref, *, n_chunks, n_heads):
    c = GLA_CHUNK
    dk = q_ref.shape[1] // n_heads
    dv = v_ref.shape[1] // n_heads
    heads = range(n_heads)

    @pl.when(pl.program_id(1) == 0)
    def _():
        st_ref[...] = jnp.zeros_like(st_ref)

    row = lax.broadcasted_iota(jnp.int32, (c, 1), 0)
    ri = lax.broadcasted_iota(jnp.int32, (c, c), 0)
    ci = lax.broadcasted_iota(jnp.int32, (c, c), 1)
    levels = (8, 16, 32)
    odd = [(row // s) % 2 == 1 for s in levels]
    same_group = [ri // (2 * s) == ci // (2 * s) for s in levels]
    hit = [(ci == 8 * (ri // 8) + jj) & (ri % 8 >= jj) for jj in range(8)]

    def chunk(ic, carry):
        rows = pl.ds(pl.multiple_of(ic * c, c), c)
        q, k, v, b = [], [], [], []
        for h in heads:
            q.append(q_ref[rows, h * dk:(h + 1) * dk].astype(F32) * (dk ** -0.5))
            k.append(k_ref[rows, h * dk:(h + 1) * dk].astype(F32))
            v.append(v_ref[rows, h * dv:(h + 1) * dv])
            b.append(b_ref[rows, h * dk:(h + 1) * dk])
        st = [st_ref[h] for h in heads]
        o = [_dot_nt((q[h] * jnp.exp2(b[h])).astype(BF16), st[h].astype(BF16)) for h in heads]
        for h in heads:
            b_last = b[h][c - 1:c, :]
            kd = (k[h] * jnp.exp2(b_last - b[h])).astype(BF16)
            st_ref[h] = st[h] * jnp.exp2(b_last) + _dot_tn(v[h], kd)
        al = []
        for h in heads:
            for i, s in enumerate(levels):
                first = _block_first(b[h], s)
                nxt = jnp.concatenate([first[s:], first[:s]], axis=0)
                qh = jnp.where(odd[i], q[h] * jnp.exp2(b[h] - first), 0.0)
                kh = jnp.where(odd[i], 0.0, k[h] * jnp.exp2(jnp.minimum(nxt - b[h], 0.0)))
                al.append(_dot_nt(qh.astype(BF16), kh.astype(BF16)))
        a = []
        for h in heads:
            ah = jnp.zeros((c, c), F32)
            for i in range(len(levels)):
                ah = ah + jnp.where(same_group[i], al[h * len(levels) + i], 0.0)
            k3 = k[h].reshape(c // 8, 8, dk)
            b3 = b[h].reshape(c // 8, 8, dk)
            for jj in range(8):
                kj = jnp.broadcast_to(k3[:, jj:jj + 1, :], k3.shape).reshape(c, dk)
                bj = jnp.broadcast_to(b3[:, jj:jj + 1, :], b3.shape).reshape(c, dk)
                tt = q[h] * kj * jnp.exp2(jnp.minimum(b[h] - bj, 0.0))
                ah = jnp.where(hit[jj], jnp.sum(tt, axis=-1, keepdims=True), ah)
            a.append(ah.astype(BF16))
        o = [o[h] + _dot(a[h], v[h]) for h in heads]
        for h in heads:
            on = _rmsnorm(o[h], ng_ref[...], GLA_NORM_EPS)
            go = go_ref[rows, h * dv:(h + 1) * dv].astype(F32)
            o_ref[rows, h * dv:(h + 1) * dv] = (on * (go * _sigmoid(go))).astype(o_ref.dtype)
        return carry

    lax.fori_loop(0, n_chunks, chunk, 0, unroll=2)


def _gla_core(y, bcum, norm_g, bsz, seq):
    nh = GLA_HEADS
    dqk = bcum.shape[1]
    dvv = (y.shape[1] - 2 * dqk) // 2
    ts = GLA_SEQ_BLOCK
    ns = seq // ts
    rowblk = lambda b, s: b * ns + s
    return pl.pallas_call(
        functools.partial(_gla_body, n_chunks=ts // GLA_CHUNK, n_heads=nh),
        grid=(bsz, ns),
        in_specs=[
            pl.BlockSpec((ts, dqk), lambda b, s: (rowblk(b, s), 0)),
            pl.BlockSpec((ts, dqk), lambda b, s: (rowblk(b, s), 1)),
            pl.BlockSpec((ts, dvv), lambda b, s: (rowblk(b, s), (2 * dqk) // dvv)),
            pl.BlockSpec((ts, dvv), lambda b, s: (rowblk(b, s), (2 * dqk) // dvv + 1)),
            pl.BlockSpec((ts, dqk), lambda b, s: (rowblk(b, s), 0)),
            _full(norm_g.shape),
        ],
        out_specs=pl.BlockSpec((ts, dvv), lambda b, s: (rowblk(b, s), 0)),
        out_shape=jax.ShapeDtypeStruct((bsz * seq, dvv), BF16),
        scratch_shapes=[pltpu.VMEM((nh, dvv // nh, dqk // nh), F32)],
        compiler_params=_params(("parallel", "arbitrary")),
        name="gla_core",
    )(y, y, y, y, bcum, norm_g)


def _gla_layer(h, g6, w_in, w_gate2, b_gate, norm_g, w_o, bsz, seq):
    d = h.shape[1]
    dk_all = w_gate2.shape[1]
    rank = w_gate2.shape[0]
    n_main = w_in.shape[1] - rank
    w_main = w_in[:, :n_main].astype(BF16)
    w_low = jnp.pad(w_in[:, n_main:], ((0, 0), (0, LANES - rank))).astype(BF16)
    w2 = jnp.pad(w_gate2, ((0, LANES - rank), (0, 0))).astype(BF16)
    y, bcum = _gla_in(h, g6, w_main, w_low, w2, b_gate[None, :])
    o = _gla_core(y, bcum, norm_g[None, :], bsz, seq)
    return o, w_o.astype(BF16), jnp.zeros((1, d), F32)


def _rwkv_pre_body(h_ref, hp_ref, g_ref, mu_ref, wrkv_ref, w1_ref, w2_ref, a1_ref, a2_ref,
                   g1_ref, g2_ref, vec_ref, r_ref, k_ref, v_ref, lw_ref, na_ref, nb_ref, gt_ref,
                   *, blocks_per_seq):
    i = pl.program_id(0)
    tm, d = h_ref.shape
    gain = g_ref[2:3, :]
    x = _rmsnorm(h_ref[...], gain)
    prev = _rmsnorm(hp_ref[...], gain)[7:8, :]
    prev = jnp.where(i % blocks_per_seq == 0, 0.0, prev)
    row = lax.broadcasted_iota(jnp.int32, (tm, 1), 0)
    xs = jnp.where(row == 0, prev, pltpu.roll(x, shift=1, axis=0))
    xx_all = xs - x
    li = lax.broadcasted_iota(jnp.int32, (LANES, LANES), 0) // RWKV_HEAD
    lj = lax.broadcasted_iota(jnp.int32, (LANES, LANES), 1) // RWKV_HEAD
    gmat = jnp.where(li == lj, 1.0, 0.0).astype(BF16)
    for r0 in range(0, tm, RWKV_PRE_SUB_ROWS):
        rs = slice(r0, r0 + RWKV_PRE_SUB_ROWS)
        xr, xx = x[rs, :], xx_all[rs, :]

        def mix(j):
            return (xr + xx * mu_ref[j:j + 1, :]).astype(BF16)

        r = _dot(mix(0), wrkv_ref[0])
        k = _dot(mix(2), wrkv_ref[1])
        v = _dot(mix(3), wrkv_ref[2])
        zw = vec_ref[0:1, :] + _dot(jnp.tanh(_dot(mix(1), w1_ref[...])).astype(BF16), w2_ref[...])
        lw = -LOG2_E * jnp.exp(-_softplus(-zw) - 0.5)
        a = _sigmoid(vec_ref[1:2, :] + _dot(_dot(mix(4), a1_ref[...]).astype(BF16), a2_ref[...]))
        gt = _dot(_sigmoid(_dot(mix(5), g1_ref[...])).astype(BF16), g2_ref[...])
        kk = k * vec_ref[2:3, :]
        r_ref[rs, :] = r.astype(r_ref.dtype)
        v_ref[rs, :] = v.astype(v_ref.dtype)
        lw_ref[rs, :] = lw
        gt_ref[rs, :] = gt.astype(gt_ref.dtype)
        k_ref[rs, :] = (k * (1.0 + (a - 1.0) * vec_ref[3:4, :])).astype(k_ref.dtype)
        for s in range(0, d, LANES):
            kks = kk[:, s:s + LANES]
            ss = _dot((kks * kks).astype(BF16), gmat)
            kkn = kks / jnp.maximum(jnp.sqrt(ss), 1e-12)
            na_ref[rs, s:s + LANES] = (-kkn).astype(na_ref.dtype)
            nb_ref[rs, s:s + LANES] = (kkn * a[:, s:s + LANES]).astype(nb_ref.dtype)


def _rwkv_pre(h, g6, mu, wrkv, w1, w2, a1, a2, g1, g2, vec, seq):
    t, d = h.shape
    tm = TOKEN_BLOCK
    tok = pl.BlockSpec((tm, d), lambda i: (i, 0))
    ins = [h, h, g6, mu, wrkv, w1, w2, a1, a2, g1, g2, vec]
    in_specs = [tok, pl.BlockSpec((8, d), lambda i: (jnp.maximum(i * (tm // 8) - 1, 0), 0))]
    in_specs += [_full(x.shape) for x in ins[2:]]
    return pl.pallas_call(
        functools.partial(_rwkv_pre_body, blocks_per_seq=seq // tm),
        grid=(t // tm,),
        in_specs=in_specs,
        out_specs=[tok] * 7,
        out_shape=[jax.ShapeDtypeStruct((t, d), F32 if i == 3 else BF16) for i in range(7)],
        compiler_params=_params(("parallel",)),
        name="rwkv_pre",
    )(*ins)


def _rwkv_body(r_ref, k_ref, v_ref, lw_ref, na_ref, nb_ref, gt_ref, vec_ref, o_ref, s_ref,
               *, n_chunks, n_pairs):
    c = RWKV_CHUNK
    hd = RWKV_HEAD
    pairs = range(n_pairs)

    @pl.when(pl.program_id(2) == 0)
    def _():
        s_ref[...] = jnp.zeros_like(s_ref)

    lane = lax.broadcasted_iota(jnp.int32, (c, LANES), 1)
    head0 = lane < hd
    ri = lax.broadcasted_iota(jnp.int32, (2 * c, 2 * c), 0)
    ci = lax.broadcasted_iota(jnp.int32, (2 * c, 2 * c), 1)
    strict = ri > ci
    incl = ri >= ci
    tri = jnp.where(lax.broadcasted_iota(jnp.int32, (c, c), 0)
                    >= lax.broadcasted_iota(jnp.int32, (c, c), 1), 1.0, 0.0).astype(BF16)

    def stack(x):
        return jnp.concatenate([jnp.where(head0, x, 0.0), jnp.where(head0, 0.0, x)], axis=0)

    def hsum(x):
        s0 = jnp.sum(jnp.where(head0, x, 0.0), axis=-1, keepdims=True)
        s1 = jnp.sum(jnp.where(head0, 0.0, x), axis=-1, keepdims=True)
        return jnp.where(head0, s0, s1)

    def chunk(ic, carry):
        r0 = pl.multiple_of(ic * c, c)
        rows = pl.ds(r0, c)
        hi, mid, lo = _split3(lw_ref[rows, :])
        cl_all = _dot(tri, hi) + _dot(tri, mid) + _dot(tri, lo)
        lhs, rhs, rhs_end, vsb, sb, decay = [], [], [], [], [], []
        for p in pairs:
            sl = slice(p * LANES, (p + 1) * LANES)
            r, k, v = (x[rows, sl].astype(F32) for x in (r_ref, k_ref, v_ref))
            lw, na, nb = lw_ref[rows, sl], na_ref[rows, sl].astype(F32), nb_ref[rows, sl].astype(F32)
            cl = cl_all[:, sl]
            cl_last = cl[c - 1:c, :]
            p_inv = jnp.exp2(-cl)
            p_end = jnp.exp2(cl_last - cl)
            lhs.append(jnp.concatenate([stack(na * jnp.exp2(cl - lw)), stack(r * jnp.exp2(cl))],
                                       axis=0).astype(BF16))
            rhs.append(jnp.concatenate([stack(nb * p_inv), stack(k * p_inv)], axis=0).astype(BF16))
            rhs_end.append(jnp.concatenate([stack(nb * p_end), stack(k * p_end)], axis=0).astype(BF16))
            vsb.append(stack(v).astype(BF16))
            sb.append(s_ref[p].astype(BF16))
            decay.append(jnp.exp2(cl_last))
        gram = [_dot_nt(lhs[p], rhs[p]) for p in pairs]
        ws = [_dot_nt(lhs[p], sb[p]) for p in pairs]
        a_ab = [jnp.where(strict, gram[p][:2 * c, :2 * c], 0.0) for p in pairs]
        a_ak = [jnp.where(strict, gram[p][:2 * c, 2 * c:], 0.0).astype(BF16) for p in pairs]
        a_r = [jnp.concatenate([jnp.where(incl, gram[p][2 * c:, :2 * c], 0.0),
                                jnp.where(incl, gram[p][2 * c:, 2 * c:], 0.0)], axis=1).astype(BF16)
               for p in pairs]
        u = [ws[p][:2 * c] + _dot(a_ak[p], vsb[p]) for p in pairs]
        apow = a_ab
        steps = (c - 1).bit_length()
        for it in range(steps):
            ab = [apow[p].astype(BF16) for p in pairs]
            if it + 1 < steps:
                both = [_dot(ab[p], jnp.concatenate([u[p].astype(BF16), ab[p]], axis=1))
                        for p in pairs]
                u = [u[p] + both[p][:, :LANES] for p in pairs]
                apow = [both[p][:, LANES:] for p in pairs]
            else:
                u = [u[p] + _dot(ab[p], u[p].astype(BF16)) for p in pairs]
        uv = [jnp.concatenate([u[p].astype(BF16), vsb[p]], axis=0) for p in pairs]
        ys = [ws[p][2 * c:] + _dot(a_r[p], uv[p]) for p in pairs]
        for p in pairs:
            s_ref[p] = s_ref[p] * decay[p] + _dot_tn(uv[p], rhs_end[p])
        for p in pairs:
            sl = slice(p * LANES, (p + 1) * LANES)
            y = ys[p][:c, :] + ys[p][c:, :]
            mean = hsum(y) * (1.0 / hd)
            yc = y - mean
            var = hsum(yc * yc) * (1.0 / hd)
            yn = yc * lax.rsqrt(var + RWKV_LNX_EPS) * vec_ref[1:2, sl] + vec_ref[2:3, sl]
            r, k, v = (x[rows, sl].astype(F32) for x in (r_ref, k_ref, v_ref))
            yn = yn + hsum(r * k * vec_ref[0:1, sl]) * v
            o_ref[rows, sl] = (yn * gt_ref[rows, sl].astype(F32)).astype(o_ref.dtype)
        return carry

    lax.fori_loop(0, n_chunks, chunk, 0, unroll=2)


def _rwkv_core(r, k, v, lw, na, nb, gt, vec, bsz, seq):
    t, d = r.shape
    ts = RWKV_SEQ_BLOCK
    ns = seq // ts
    width = RWKV_PAIRS * LANES
    tok = pl.BlockSpec((ts, width), lambda b, p, s: (b * ns + s, p))
    return pl.pallas_call(
        functools.partial(_rwkv_body, n_chunks=ts // RWKV_CHUNK, n_pairs=RWKV_PAIRS),
        grid=(bsz, d // width, ns),
        in_specs=[tok] * 7 + [pl.BlockSpec((vec.shape[0], width), lambda b, p, s: (0, p))],
        out_specs=tok,
        out_shape=jax.ShapeDtypeStruct((t, d), BF16),
        scratch_shapes=[pltpu.VMEM((RWKV_PAIRS, LANES, LANES), F32)],
        compiler_params=_params(("parallel", "parallel", "arbitrary")),
        name="rwkv_core",
    )(r, k, v, lw, na, nb, gt, vec)


def _pad_to(x, axis, size):
    pads = [(0, 0)] * x.ndim
    pads[axis] = (0, size - x.shape[axis])
    return jnp.pad(x, pads)


def _rwkv_layer(h, g6, mu, w_rkv, w0, w1, w2, a0, a1, a2, g1, g2, k_k, k_a, r_k, lnx_g, lnx_b,
                w_o, bsz, seq):
    d = h.shape[1]
    up = lambda n: -(-n // LANES) * LANES
    w1p = _pad_to(w1, 1, up(w1.shape[1])).astype(BF16)
    w2p = _pad_to(w2, 0, up(w2.shape[0])).astype(BF16)
    a1p = _pad_to(a1, 1, up(a1.shape[1])).astype(BF16)
    a2p = _pad_to(a2, 0, up(a2.shape[0])).astype(BF16)
    g1p = _pad_to(g1, 1, up(g1.shape[1])).astype(BF16)
    g2p = _pad_to(g2, 0, up(g2.shape[0])).astype(BF16)
    vec_pre = _pad_to(jnp.stack([w0, a0, k_k, k_a]), 0, 8)
    outs = _rwkv_pre(h, g6, _pad_to(mu, 0, 8), w_rkv.astype(BF16), w1p, w2p, a1p, a2p, g1p, g2p,
                     vec_pre, seq)
    vec_core = _pad_to(jnp.stack([r_k.reshape(d), lnx_g, lnx_b]), 0, 8)
    o = _rwkv_core(*outs, vec_core, bsz, seq)
    return o, w_o.astype(BF16), jnp.zeros((1, d), F32)


def kernel(x, norm_g, ffn_w_in, ffn_w_out, swa_w_qkv, swa_b_qkv, swa_sinks, swa_w_o, swa_b_o, gla_w_in, gla_w_gate2, gla_b_gate, gla_norm_g, gla_w_o, rwkv_mu, rwkv_w_rkv, rwkv_w0, rwkv_w1, rwkv_w2, rwkv_a0, rwkv_a1, rwkv_a2, rwkv_g1, rwkv_g2, rwkv_k_k, rwkv_k_a, rwkv_r_k, rwkv_lnx_g, rwkv_lnx_b, rwkv_w_o):
    bsz, seq, d = x.shape
    depth = norm_g.shape[0]
    h = x.reshape(bsz * seq, d)
    win, wout = ffn_w_in.astype(BF16), ffn_w_out.astype(BF16)
    for layer in range(depth):
        g6 = norm_g[layer]
        h = _ffn(h, g6, win, wout, layer, 0)
        kind, j = layer % 3, layer // 3
        if kind == 0:
            mix = _swa_layer(h, g6, swa_w_qkv[j], swa_b_qkv[j], swa_sinks[j], swa_w_o[j],
                             swa_b_o[j], bsz, seq)
        elif kind == 1:
            mix = _gla_layer(h, g6, gla_w_in[j], gla_w_gate2[j], gla_b_gate[j], gla_norm_g[j],
                             gla_w_o[j], bsz, seq)
        else:
            mix = _rwkv_layer(h, g6, rwkv_mu[j], rwkv_w_rkv[j], rwkv_w0[j], rwkv_w1[j], rwkv_w2[j],
                              rwkv_a0[j], rwkv_a1[j], rwkv_a2[j], rwkv_g1[j], rwkv_g2[j],
                              rwkv_k_k[j], rwkv_k_a[j], rwkv_r_k[j], rwkv_lnx_g[j], rwkv_lnx_b[j],
                              rwkv_w_o[j], bsz, seq)
        h = _ffn(h, g6, win, wout, layer, 1, mixer=mix)
    return h.reshape(bsz, seq, d)
```

```python
import functools

import jax
import jax.numpy as jnp
from jax import lax
from jax.experimental import pallas as pl
from jax.experimental.pallas import tpu as pltpu

F32 = jnp.float32
BF16 = jnp.bfloat16

NORM_EPS = 1e-6
GLA_NORM_EPS = 1e-5
GLA_GATE_TEMP = 16.0
RWKV_LNX_EPS = 64e-5
LOG2_E = 1.4426950408889634

D_FF_CHUNK = 256
TOKEN_BLOCK = 512
FFN_TOKEN_BLOCK = 1024
FFN_SUB_ROWS = 512
SWA_BLOCK = 128
SWA_Q_BLOCKS = 4
SWA_HEAD_DIM = 64
SWA_HEADS = 16
SWA_KV_HEADS = 4
GLA_HEADS = 4
GLA_CHUNK = 64
GLA_SEQ_BLOCK = 512
RWKV_HEAD = 64
RWKV_CHUNK = 64
RWKV_SEQ_BLOCK = 512
RWKV_PAIRS = 8
LANES = 128
VMEM_LIMIT = 56 * 1024 * 1024


def _params(sem):
    return pltpu.CompilerParams(dimension_semantics=sem, vmem_limit_bytes=VMEM_LIMIT)


def _rmsnorm(x, g, eps=NORM_EPS):
    return x * lax.rsqrt(jnp.mean(x * x, axis=-1, keepdims=True) + eps) * g


def _sigmoid(x):
    return 1.0 / (1.0 + jnp.exp(-x))


def _softplus(x):
    return jnp.maximum(x, 0.0) + jnp.log(1.0 + jnp.exp(-jnp.abs(x)))


def _dot(a, b):
    return jnp.dot(a, b, preferred_element_type=F32)


def _dot_nt(a, b):
    return lax.dot_general(a, b, (((1,), (1,)), ((), ())), preferred_element_type=F32)


def _dot_tn(a, b):
    return lax.dot_general(a, b, (((0,), (0,)), ((), ())), preferred_element_type=F32)


def _full(shape):
    n = len(shape)
    return pl.BlockSpec(shape, lambda *_: (0,) * n)


def _ffn_body(*refs, g_row, mixer_proj):
    if mixer_proj:
        h_ref, a_ref, wo_ref, bo_ref, g_ref, win_ref, wout_ref, o_ref, xn_ref, acc_ref = refs
    else:
        h_ref, g_ref, win_ref, wout_ref, o_ref, xn_ref, acc_ref = refs
    d_ff = wout_ref.shape[0]
    tf = D_FF_CHUNK
    tm = h_ref.shape[0]
    subs = [slice(r0, r0 + FFN_SUB_ROWS) for r0 in range(0, tm, FFN_SUB_ROWS)]
    for rs in subs:
        hs = h_ref[rs, :]
        if mixer_proj:
            m = _dot(a_ref[rs, :], wo_ref[...]) + bo_ref[...]
            hs = hs + _rmsnorm(m, g_ref[g_row - 1:g_row, :])
            o_ref[rs, :] = hs
        xn_ref[rs, :] = _rmsnorm(hs, g_ref[g_row:g_row + 1, :]).astype(BF16)
    res_ref = o_ref if mixer_proj else h_ref
    for rs in subs:
        for c0 in range(0, d_ff, tf):
            xn = xn_ref[rs, :]
            gate = _dot(xn, win_ref[:, c0:c0 + tf])
            up = _dot(xn, win_ref[:, d_ff + c0:d_ff + c0 + tf])
            act = (gate * _sigmoid(gate) * up).astype(BF16)
            part = _dot(act, wout_ref[c0:c0 + tf, :])
            if c0 == 0:
                acc_ref[rs, :] = part
            else:
                acc_ref[rs, :] += part
    for rs in subs:
        o_ref[rs, :] = res_ref[rs, :] + 0.5 * _rmsnorm(acc_ref[rs, :], g_ref[g_row + 1:g_row + 2, :])


def _resident(shape):
    n = len(shape)
    return pl.BlockSpec(shape, lambda *_: (0,) * n, pipeline_mode=pl.Buffered(1))


def _stacked_weight(w, *lead):
    idx = tuple(lead) + (0, 0)
    return pl.BlockSpec((None,) * len(lead) + w.shape[-2:], lambda *_: idx,
                        pipeline_mode=pl.Buffered(1))


def _ffn(h, g6, win, wout, layer, which, mixer=None):
    t, d = h.shape
    g_row = 4 * which
    tm = FFN_TOKEN_BLOCK
    tok = pl.BlockSpec((tm, d), lambda i: (i, 0))
    ins, in_specs = [h], [tok]
    if mixer is not None:
        a, w_o, b_o = mixer
        ins += [a, w_o, b_o]
        in_specs += [pl.BlockSpec((tm, a.shape[1]), lambda i: (i, 0)), _resident(w_o.shape),
                     _full(b_o.shape)]
    ins += [g6, win, wout]
    in_specs += [_full(g6.shape), _stacked_weight(win, layer, which),
                 _stacked_weight(wout, layer, which)]
    return pl.pallas_call(
        functools.partial(_ffn_body, g_row=g_row, mixer_proj=mixer is not None),
        grid=(t // tm,),
        in_specs=in_specs,
        out_specs=tok,
        out_shape=jax.ShapeDtypeStruct((t, d), F32),
        scratch_shapes=[pltpu.VMEM((tm, d), BF16), pltpu.VMEM((tm, d), F32)],
        compiler_params=_params(("parallel",)),
        name="ffn_mix" if mixer is not None else "ffn",
    )(*ins)


def _norm_proj_body(h_ref, g_ref, w_ref, b_ref, *o_refs, g_row, n_chunk):
    xn = _rmsnorm(h_ref[...], g_ref[g_row:g_row + 1, :]).astype(BF16)
    col = 0
    for o_ref in o_refs:
        n = o_ref.shape[1]
        for c in range(0, n, n_chunk):
            cs = slice(col + c, col + min(c + n_chunk, n))
            y = _dot(xn, w_ref[:, cs]) + b_ref[:, cs]
            o_ref[:, c:c + cs.stop - cs.start] = y.astype(o_ref.dtype)
        col += n


def _norm_proj(h, g6, g_row, w, b, widths):
    t, d = h.shape
    tm = TOKEN_BLOCK
    return pl.pallas_call(
        functools.partial(_norm_proj_body, g_row=g_row, n_chunk=512),
        grid=(t // tm,),
        in_specs=[
            pl.BlockSpec((tm, d), lambda i: (i, 0)),
            _full(g6.shape),
            _resident(w.shape),
            _full(b.shape),
        ],
        out_specs=[pl.BlockSpec((tm, n), lambda i: (i, 0)) for n in widths],
        out_shape=[jax.ShapeDtypeStruct((t, n), BF16) for n in widths],
        compiler_params=_params(("parallel",)),
        name="norm_proj",
    )(h, g6, w, b)


def _swa_body(sink_ref, q_ref, kp_ref, kc_ref, vp_ref, vc_ref, o_ref):
    first_step = pl.program_id(1) == 0
    blk = SWA_BLOCK
    nkv = SWA_KV_HEADS
    nq = q_ref.shape[0] // blk
    kall = jnp.concatenate([kp_ref[...], kc_ref[...]], axis=0)
    vall = jnp.concatenate([vp_ref[...], vc_ref[...]], axis=0)
    own = (lax.broadcasted_iota(jnp.int32, (blk, blk), 1)
           <= lax.broadcasted_iota(jnp.int32, (blk, blk), 0))
    no_prev = jnp.where(first_step, -jnp.inf, 0.0)
    width = nkv * SWA_HEAD_DIM
    lane_head = lax.broadcasted_iota(jnp.int32, kall.shape, 1) // SWA_HEAD_DIM
    scale = SWA_HEAD_DIM ** -0.5
    groups = SWA_HEADS // nkv
    kmask = [kall * jnp.where(lane_head == hh, scale, 0.0).astype(BF16) for hh in range(nkv)]
    vmask = [vall * jnp.where(lane_head == hh, 1.0, 0.0).astype(BF16) for hh in range(nkv)]
    work = [(j, s) for j in range(nq) for s in range(groups)]

    def keys_of(masked, j):
        return jnp.concatenate([m[j * blk:(j + 2) * blk] for m in masked], axis=0)

    def softmax(j, s, sc4):
        ps = []
        for hh in range(nkv):
            c0 = hh * 2 * blk
            prev = sc4[:, c0:c0 + blk]
            if j == 0:
                prev = prev + no_prev
            sc = jnp.where(own, sc4[:, c0 + blk:c0 + 2 * blk], prev)
            sink = sink_ref[hh * groups + s]
            m = jnp.maximum(jnp.max(sc, axis=-1, keepdims=True), sink)
            p = jnp.exp(sc - m)
            inv = 1.0 / (jnp.sum(p, axis=-1, keepdims=True) + jnp.exp(sink - m))
            p = p * inv
            ps.append(jnp.where(own, 0.0, p).astype(BF16))
            ps.append(jnp.where(own, p, 0.0).astype(BF16))
        return jnp.concatenate(ps, axis=1)

    kstack = [keys_of(kmask, j) for j in range(nq)]
    vstack = [keys_of(vmask, j) for j in range(nq)]
    scores = [_dot_nt(q_ref[j * blk:(j + 1) * blk, s * width:(s + 1) * width], kstack[j])
              for j, s in work]
    probs = [softmax(j, s, scores[i]) for i, (j, s) in enumerate(work)]
    for i, (j, s) in enumerate(work):
        o_ref[j * blk:(j + 1) * blk, s * width:(s + 1) * width] = (
            _dot(probs[i], vstack[j]).astype(o_ref.dtype))


def _swa_core(q, k, v, sinks, bsz, seq):
    blk = SWA_BLOCK
    nq = SWA_Q_BLOCKS
    ns = seq // (nq * blk)
    dq, dkv = q.shape[1], k.shape[1]
    cur = lambda b, n: (b * ns + n, 0)
    prev = lambda b, n: ((b * ns + n) * nq - jnp.minimum(n, 1), 0)
    return pl.pallas_call(
        _swa_body,
        grid=(bsz, ns),
        in_specs=[
            pl.BlockSpec(memory_space=pltpu.SMEM),
            pl.BlockSpec((nq * blk, dq), cur),
            pl.BlockSpec((blk, dkv), prev),
            pl.BlockSpec((nq * blk, dkv), cur),
            pl.BlockSpec((blk, dkv), prev),
            pl.BlockSpec((nq * blk, dkv), cur),
        ],
        out_specs=pl.BlockSpec((nq * blk, dq), cur),
        out_shape=jax.ShapeDtypeStruct((bsz * seq, dq), BF16),
        compiler_params=_params(("parallel", "parallel")),
        name="swa_core",
    )(sinks, q, k, k, v, v)


def _swa_layer(h, g6, w_qkv, b_qkv, sinks, w_o, b_o, bsz, seq):
    d = h.shape[1]
    hd, nh, nkv = SWA_HEAD_DIM, SWA_HEADS, SWA_KV_HEADS
    grp = nh // nkv
    dq = nh * hd
    wq = w_qkv[:, :dq].reshape(d, nkv, grp, hd).transpose(0, 2, 1, 3).reshape(d, dq)
    bq = b_qkv[:dq].reshape(nkv, grp, hd).transpose(1, 0, 2).reshape(dq)
    w = jnp.concatenate([wq, w_qkv[:, dq:]], axis=1).astype(BF16)
    b = jnp.concatenate([bq, b_qkv[dq:]])[None, :]
    wo = w_o.reshape(nkv, grp, hd, d).transpose(1, 0, 2, 3).reshape(dq, d).astype(BF16)
    dkv = nkv * hd
    q, k, v = _norm_proj(h, g6, 2, w, b, (dq, dkv, dkv))
    att = _swa_core(q, k, v, sinks, bsz, seq)
    return att, wo, b_o[None, :]


def _split3(x):
    hi = x.astype(BF16)
    r1 = x - hi.astype(F32)
    mid = r1.astype(BF16)
    lo = (r1 - mid.astype(F32)).astype(BF16)
    return hi, mid, lo


def _gla_in_body(h_ref, g_ref, w_ref, wl_ref, w2_ref, bg_ref, y_ref, b_ref, *, n_chunk):
    tm = h_ref.shape[0]
    c = GLA_CHUNK
    xn = _rmsnorm(h_ref[...], g_ref[2:3, :]).astype(BF16)
    for c0 in range(0, w_ref.shape[1], n_chunk):
        y_ref[:, c0:c0 + n_chunk] = _dot(xn, w_ref[:, c0:c0 + n_chunk]).astype(y_ref.dtype)
    g_low = _dot(xn, wl_ref[...]).astype(BF16)
    z = _dot(g_low, w2_ref[...]) + bg_ref[...]
    la = -_softplus(-z) * (LOG2_E / GLA_GATE_TEMP)
    tri = jnp.where(lax.broadcasted_iota(jnp.int32, (c, c), 0)
                    >= lax.broadcasted_iota(jnp.int32, (c, c), 1), 1.0, 0.0).astype(BF16)
    for r0 in range(0, tm, c):
        hi, mid, lo = _split3(la[r0:r0 + c, :])
        b_ref[r0:r0 + c, :] = _dot(tri, hi) + _dot(tri, mid) + _dot(tri, lo)


def _gla_in(h, g6, w_main, w_low, w_gate2, b_gate):
    t, d = h.shape
    n = w_main.shape[1]
    dk = w_gate2.shape[1]
    tm = TOKEN_BLOCK
    return pl.pallas_call(
        functools.partial(_gla_in_body, n_chunk=512),
        grid=(t // tm,),
        in_specs=[
            pl.BlockSpec((tm, d), lambda i: (i, 0)),
            _full(g6.shape),
            _resident(w_main.shape),
            _resident(w_low.shape),
            _resident(w_gate2.shape),
            _full(b_gate.shape),
        ],
        out_specs=[pl.BlockSpec((tm, n), lambda i: (i, 0)), pl.BlockSpec((tm, dk), lambda i: (i, 0))],
        out_shape=[jax.ShapeDtypeStruct((t, n), BF16), jax.ShapeDtypeStruct((t, dk), F32)],
        compiler_params=_params(("parallel",)),
        name="gla_in",
    )(h, g6, w_main, w_low, w_gate2, b_gate)


def _block_first(x, s):
    c, l = x.shape
    xb = x.reshape(c // s, s, l)
    return jnp.broadcast_to(xb[:, 0:1, :], (c // s, s, l)).reshape(c, l)


def _gla_body(q_ref, k_ref, v_ref, go_ref, b_ref, ng_ref, o_ref, st_ref, *, n_chunks, n_heads):
    c = GLA_CHUNK
    dk = q_ref.shape[1] // n_heads
    dv = v_ref.shape[1] // n_heads
    heads = range(n_heads)

    @pl.when(pl.program_id(1) == 0)
    def _():
        st_ref[...] = jnp.zeros_like(st_ref)

    row = lax.broadcasted_iota(jnp.int32, (c, 1), 0)
    ri = lax.broadcasted_iota(jnp.int32, (c, c), 0)
    ci = lax.broadcasted_iota(jnp.int32, (c, c), 1)
    levels = (8, 16, 32)
    odd = [(row // s) % 2 == 1 for s in levels]
    same_group = [ri // (2 * s) == ci // (2 * s) for s in levels]
    hit = [(ci == 8 * (ri // 8) + jj) & (ri % 8 >= jj) for jj in range(8)]

    def chunk(ic, carry):
        rows = pl.ds(pl.multiple_of(ic * c, c), c)
        q, k, v, b = [], [], [], []
        for h in heads:
            q.append(q_ref[rows, h * dk:(h + 1) * dk].astype(F32) * (dk ** -0.5))
            k.append(k_ref[rows, h * dk:(h + 1) * dk].astype(F32))
            v.append(v_ref[rows, h * dv:(h + 1) * dv])
            b.append(b_ref[rows, h * dk:(h + 1) * dk])
        st = [st_ref[h] for h in heads]
        o = [_dot_nt((q[h] * jnp.exp2(b[h])).astype(BF16), st[h].astype(BF16)) for h in heads]
        for h in heads:
            b_last = b[h][c - 1:c, :]
            kd = (k[h] * jnp.exp2(b_last - b[h])).astype(BF16)
            st_ref[h] = st[h] * jnp.exp2(b_last) + _dot_tn(v[h], kd)
        al = []
        for h in heads:
            for i, s in enumerate(levels):
                first = _block_first(b[h], s)
                nxt = jnp.concatenate([first[s:], first[:s]], axis=0)
                qh = jnp.where(odd[i], q[h] * jnp.exp2(b[h] - first), 0.0)
                kh = jnp.where(odd[i], 0.0, k[h] * jnp.exp2(jnp.minimum(nxt - b[h], 0.0)))
                al.append(_dot_nt(qh.astype(BF16), kh.astype(BF16)))
        a = []
        for h in heads:
            ah = jnp.zeros((c, c), F32)
            for i in range(len(levels)):
                ah = ah + jnp.where(same_group[i], al[h * len(levels) + i], 0.0)
            k3 = k[h].reshape(c // 8, 8, dk)
            b3 = b[h].reshape(c // 8, 8, dk)
            for jj in range(8):
                kj = jnp.broadcast_to(k3[:, jj:jj + 1, :], k3.shape).reshape(c, dk)
                bj = jnp.broadcast_to(b3[:, jj:jj + 1, :], b3.shape).reshape(c, dk)
                tt = q[h] * kj * jnp.exp2(jnp.minimum(b[h] - bj, 0.0))
                ah = jnp.where(hit[jj], jnp.sum(tt, axis=-1, keepdims=True), ah)
            a.append(ah.astype(BF16))
        o = [o[h] + _dot(a[h], v[h]) for h in heads]
        for h in heads:
            on = _rmsnorm(o[h], ng_ref[...], GLA_NORM_EPS)
            go = go_ref[rows, h * dv:(h + 1) * dv].astype(F32)
            o_ref[rows, h * dv:(h + 1) * dv] = (on * (go * _sigmoid(go))).astype(o_ref.dtype)
        return carry

    lax.fori_loop(0, n_chunks, chunk, 0, unroll=2)


def _gla_core(y, bcum, norm_g, bsz, seq):
    nh = GLA_HEADS
    dqk = bcum.shape[1]
    dvv = (y.shape[1] - 2 * dqk) // 2
    ts = GLA_SEQ_BLOCK
    ns = seq // ts
    rowblk = lambda b, s: b * ns + s
    return pl.pallas_call(
        functools.partial(_gla_body, n_chunks=ts // GLA_CHUNK, n_heads=nh),
        grid=(bsz, ns),
        in_specs=[
            pl.BlockSpec((ts, dqk), lambda b, s: (rowblk(b, s), 0)),
            pl.BlockSpec((ts, dqk), lambda b, s: (rowblk(b, s), 1)),
            pl.BlockSpec((ts, dvv), lambda b, s: (rowblk(b, s), (2 * dqk) // dvv)),
            pl.BlockSpec((ts, dvv), lambda b, s: (rowblk(b, s), (2 * dqk) // dvv + 1)),
            pl.BlockSpec((ts, dqk), lambda b, s: (rowblk(b, s), 0)),
            _full(norm_g.shape),
        ],
        out_specs=pl.BlockSpec((ts, dvv), lambda b, s: (rowblk(b, s), 0)),
        out_shape=jax.ShapeDtypeStruct((bsz * seq, dvv), BF16),
        scratch_shapes=[pltpu.VMEM((nh, dvv // nh, dqk // nh), F32)],
        compiler_params=_params(("parallel", "arbitrary")),
        name="gla_core",
    )(y, y, y, y, bcum, norm_g)


def _gla_layer(h, g6, w_in, w_gate2, b_gate, norm_g, w_o, bsz, seq):
    d = h.shape[1]
    rank = w_gate2.shape[0]
    n_main = w_in.shape[1] - rank
    w_main = w_in[:, :n_main].astype(BF16)
    w_low = jnp.pad(w_in[:, n_main:], ((0, 0), (0, LANES - rank))).astype(BF16)
    w2 = jnp.pad(w_gate2, ((0, LANES - rank), (0, 0))).astype(BF16)
    y, bcum = _gla_in(h, g6, w_main, w_low, w2, b_gate[None, :])
    o = _gla_core(y, bcum, norm_g[None, :], bsz, seq)
    return o, w_o.astype(BF16), jnp.zeros((1, d), F32)


def _rwkv_pre_body(h_ref, hp_ref, g_ref, mu_ref, wrkv_ref, w1_ref, w2_ref, a1_ref, a2_ref,
                   g1_ref, g2_ref, vec_ref, r_ref, k_ref, v_ref, lw_ref, na_ref, nb_ref, gt_ref,
                   *, blocks_per_seq):
    i = pl.program_id(0)
    tm, d = h_ref.shape
    gain = g_ref[2:3, :]
    x = _rmsnorm(h_ref[...], gain)
    prev = _rmsnorm(hp_ref[...], gain)[7:8, :]
    prev = jnp.where(i % blocks_per_seq == 0, 0.0, prev)
    row = lax.broadcasted_iota(jnp.int32, (tm, 1), 0)
    xs = jnp.where(row == 0, prev, pltpu.roll(x, shift=1, axis=0))
    xx = xs - x
    li = lax.broadcasted_iota(jnp.int32, (LANES, LANES), 0) // RWKV_HEAD
    lj = lax.broadcasted_iota(jnp.int32, (LANES, LANES), 1) // RWKV_HEAD
    gmat = jnp.where(li == lj, 1.0, 0.0).astype(BF16)

    def mix(j):
        return (x + xx * mu_ref[j:j + 1, :]).astype(BF16)

    r = _dot(mix(0), wrkv_ref[0])
    k = _dot(mix(2), wrkv_ref[1])
    v = _dot(mix(3), wrkv_ref[2])
    zw = vec_ref[0:1, :] + _dot(jnp.tanh(_dot(mix(1), w1_ref[...])).astype(BF16), w2_ref[...])
    lw = -LOG2_E * jnp.exp(-_softplus(-zw) - 0.5)
    a = _sigmoid(vec_ref[1:2, :] + _dot(_dot(mix(4), a1_ref[...]).astype(BF16), a2_ref[...]))
    gt = _dot(_sigmoid(_dot(mix(5), g1_ref[...])).astype(BF16), g2_ref[...])
    kk = k * vec_ref[2:3, :]
    r_ref[...] = r.astype(r_ref.dtype)
    v_ref[...] = v.astype(v_ref.dtype)
    lw_ref[...] = lw
    gt_ref[...] = gt.astype(gt_ref.dtype)
    k_ref[...] = (k * (1.0 + (a - 1.0) * vec_ref[3:4, :])).astype(k_ref.dtype)
    for s in range(0, d, LANES):
        kks = kk[:, s:s + LANES]
        ss = _dot((kks * kks).astype(BF16), gmat)
        kkn = kks / jnp.maximum(jnp.sqrt(ss), 1e-12)
        na_ref[:, s:s + LANES] = (-kkn).astype(na_ref.dtype)
        nb_ref[:, s:s + LANES] = (kkn * a[:, s:s + LANES]).astype(nb_ref.dtype)


def _rwkv_pre(h, g6, mu, wrkv, w1, w2, a1, a2, g1, g2, vec, seq):
    t, d = h.shape
    tm = TOKEN_BLOCK
    tok = pl.BlockSpec((tm, d), lambda i: (i, 0))
    ins = [h, h, g6, mu, wrkv, w1, w2, a1, a2, g1, g2, vec]
    in_specs = [tok, pl.BlockSpec((8, d), lambda i: (jnp.maximum(i * (tm // 8) - 1, 0), 0))]
    in_specs += [_full(x.shape) for x in ins[2:]]
    return pl.pallas_call(
        functools.partial(_rwkv_pre_body, blocks_per_seq=seq // tm),
        grid=(t // tm,),
        in_specs=in_specs,
        out_specs=[tok] * 7,
        out_shape=[jax.ShapeDtypeStruct((t, d), F32 if i == 3 else BF16) for i in range(7)],
        compiler_params=_params(("parallel",)),
        name="rwkv_pre",
    )(*ins)


def _rwkv_body(r_ref, k_ref, v_ref, lw_ref, na_ref, nb_ref, gt_ref, vec_ref, o_ref, s_ref,
               *, n_chunks, n_pairs):
    c = RWKV_CHUNK
    hd = RWKV_HEAD
    pairs = range(n_pairs)

    @pl.when(pl.program_id(2) == 0)
    def _():
        s_ref[...] = jnp.zeros_like(s_ref)

    lane = lax.broadcasted_iota(jnp.int32, (c, LANES), 1)
    head0 = lane < hd
    ri = lax.broadcasted_iota(jnp.int32, (2 * c, 2 * c), 0)
    ci = lax.broadcasted_iota(jnp.int32, (2 * c, 2 * c), 1)
    strict = ri > ci
    incl = ri >= ci
    tri = jnp.where(lax.broadcasted_iota(jnp.int32, (c, c), 0)
                    >= lax.broadcasted_iota(jnp.int32, (c, c), 1), 1.0, 0.0).astype(BF16)

    def stack(x):
        return jnp.concatenate([jnp.where(head0, x, 0.0), jnp.where(head0, 0.0, x)], axis=0)

    def hsum(x):
        s0 = jnp.sum(jnp.where(head0, x, 0.0), axis=-1, keepdims=True)
        s1 = jnp.sum(jnp.where(head0, 0.0, x), axis=-1, keepdims=True)
        return jnp.where(head0, s0, s1)

    def chunk(ic, carry):
        r0 = pl.multiple_of(ic * c, c)
        rows = pl.ds(r0, c)
        hi, mid, lo = _split3(lw_ref[rows, :])
        cl_all = _dot(tri, hi) + _dot(tri, mid) + _dot(tri, lo)
        lhs, lhs_flat, rhs, rhs_end, vsb, sb, decay = [], [], [], [], [], [], []
        for p in pairs:
            sl = slice(p * LANES, (p + 1) * LANES)
            r, k, v = (x[rows, sl].astype(F32) for x in (r_ref, k_ref, v_ref))
            lw, na, nb = lw_ref[rows, sl], na_ref[rows, sl].astype(F32), nb_ref[rows, sl].astype(F32)
            cl = cl_all[:, sl]
            cl_last = cl[c - 1:c, :]
            p_inv = jnp.exp2(-cl)
            p_end = jnp.exp2(cl_last - cl)
            at, rt = na * jnp.exp2(cl - lw), r * jnp.exp2(cl)
            lhs.append(jnp.concatenate([stack(at), stack(rt)], axis=0).astype(BF16))
            lhs_flat.append(jnp.concatenate([at, rt], axis=0).astype(BF16))
            rhs.append(jnp.concatenate([stack(nb * p_inv), stack(k * p_inv)], axis=0).astype(BF16))
            rhs_end.append(jnp.concatenate([stack(nb * p_end), stack(k * p_end)], axis=0).astype(BF16))
            vsb.append(stack(v).astype(BF16))
            sb.append(s_ref[p].astype(BF16))
            decay.append(jnp.exp2(cl_last))
        gram = [_dot_nt(lhs[p], rhs[p]) for p in pairs]
        ws = [_dot_nt(lhs_flat[p], sb[p]) for p in pairs]
        a_ab = [jnp.where(strict, gram[p][:2 * c, :2 * c], 0.0) for p in pairs]
        a_ak = [jnp.where(strict, gram[p][:2 * c, 2 * c:], 0.0).astype(BF16) for p in pairs]
        a_r = [jnp.concatenate([jnp.where(incl, gram[p][2 * c:, :2 * c], 0.0),
                                jnp.where(incl, gram[p][2 * c:, 2 * c:], 0.0)], axis=1).astype(BF16)
               for p in pairs]
        u = [stack(ws[p][:c]) + _dot(a_ak[p], vsb[p]) for p in pairs]
        apow = a_ab
        steps = (c - 1).bit_length()
        for it in range(steps):
            ab = [apow[p].astype(BF16) for p in pairs]
            if it + 1 < steps:
                both = [_dot(ab[p], jnp.concatenate([u[p].astype(BF16), ab[p]], axis=1))
                        for p in pairs]
                u = [u[p] + both[p][:, :LANES] for p in pairs]
                apow = [both[p][:, LANES:] for p in pairs]
            else:
                u = [u[p] + _dot(ab[p], u[p].astype(BF16)) for p in pairs]
        uv = [jnp.concatenate([u[p].astype(BF16), vsb[p]], axis=0) for p in pairs]
        ys = [_dot(a_r[p], uv[p]) for p in pairs]
        for p in pairs:
            s_ref[p] = s_ref[p] * decay[p] + _dot_tn(uv[p], rhs_end[p])
        for p in pairs:
            sl = slice(p * LANES, (p + 1) * LANES)
            y = ys[p][:c, :] + ys[p][c:, :] + ws[p][c:, :]
            mean = hsum(y) * (1.0 / hd)
            yc = y - mean
            var = hsum(yc * yc) * (1.0 / hd)
            yn = yc * lax.rsqrt(var + RWKV_LNX_EPS) * vec_ref[1:2, sl] + vec_ref[2:3, sl]
            r, k, v = (x[rows, sl].astype(F32) for x in (r_ref, k_ref, v_ref))
            yn = yn + hsum(r * k * vec_ref[0:1, sl]) * v
            o_ref[rows, sl] = (yn * gt_ref[rows, sl].astype(F32)).astype(o_ref.dtype)
        return carry

    lax.fori_loop(0, n_chunks, chunk, 0, unroll=2)


def _rwkv_core(r, k, v, lw, na, nb, gt, vec, bsz, seq):
    t, d = r.shape
    ts = RWKV_SEQ_BLOCK
    ns = seq // ts
    width = RWKV_PAIRS * LANES
    tok = pl.BlockSpec((ts, width), lambda b, p, s: (b * ns + s, p))
    return pl.pallas_call(
        functools.partial(_rwkv_body, n_chunks=ts // RWKV_CHUNK, n_pairs=RWKV_PAIRS),
        grid=(bsz, d // width, ns),
        in_specs=[tok] * 7 + [pl.BlockSpec((vec.shape[0], width), lambda b, p, s: (0, p))],
        out_specs=tok,
        out_shape=jax.ShapeDtypeStruct((t, d), BF16),
        scratch_shapes=[pltpu.VMEM((RWKV_PAIRS, LANES, LANES), F32)],
        compiler_params=_params(("parallel", "parallel", "arbitrary")),
        name="rwkv_core",
    )(r, k, v, lw, na, nb, gt, vec)


def _pad_to(x, axis, size):
    pads = [(0, 0)] * x.ndim
    pads[axis] = (0, size - x.shape[axis])
    return jnp.pad(x, pads)


def _rwkv_layer(h, g6, mu, w_rkv, w0, w1, w2, a0, a1, a2, g1, g2, k_k, k_a, r_k, lnx_g, lnx_b,
                w_o, bsz, seq):
    d = h.shape[1]
    up = lambda n: -(-n // LANES) * LANES
    w1p = _pad_to(w1, 1, up(w1.shape[1])).astype(BF16)
    w2p = _pad_to(w2, 0, up(w2.shape[0])).astype(BF16)
    a1p = _pad_to(a1, 1, up(a1.shape[1])).astype(BF16)
    a2p = _pad_to(a2, 0, up(a2.shape[0])).astype(BF16)
    g1p = _pad_to(g1, 1, up(g1.shape[1])).astype(BF16)
    g2p = _pad_to(g2, 0, up(g2.shape[0])).astype(BF16)
    vec_pre = _pad_to(jnp.stack([w0, a0, k_k, k_a]), 0, 8)
    outs = _rwkv_pre(h, g6, _pad_to(mu, 0, 8), w_rkv.astype(BF16), w1p, w2p, a1p, a2p, g1p, g2p,
                     vec_pre, seq)
    vec_core = _pad_to(jnp.stack([r_k.reshape(d), lnx_g, lnx_b]), 0, 8)
    o = _rwkv_core(*outs, vec_core, bsz, seq)
    return o, w_o.astype(BF16), jnp.zeros((1, d), F32)


def kernel(x, norm_g, ffn_w_in, ffn_w_out, swa_w_qkv, swa_b_qkv, swa_sinks, swa_w_o, swa_b_o, gla_w_in, gla_w_gate2, gla_b_gate, gla_norm_g, gla_w_o, rwkv_mu, rwkv_w_rkv, rwkv_w0, rwkv_w1, rwkv_w2, rwkv_a0, rwkv_a1, rwkv_a2, rwkv_g1, rwkv_g2, rwkv_k_k, rwkv_k_a, rwkv_r_k, rwkv_lnx_g, rwkv_lnx_b, rwkv_w_o):
    bsz, seq, d = x.shape
    depth = norm_g.shape[0]
    h = x.reshape(bsz * seq, d)
    win, wout = ffn_w_in.astype(BF16), ffn_w_out.astype(BF16)
    for layer in range(depth):
        g6 = norm_g[layer]
        h = _ffn(h, g6, win, wout, layer, 0)
        kind, j = layer % 3, layer // 3
        if kind == 0:
            mix = _swa_layer(h, g6, swa_w_qkv[j], swa_b_qkv[j], swa_sinks[j], swa_w_o[j],
                             swa_b_o[j], bsz, seq)
        elif kind == 1:
            mix = _gla_layer(h, g6, gla_w_in[j], gla_w_gate2[j], gla_b_gate[j], gla_norm_g[j],
                             gla_w_o[j], bsz, seq)
        else:
            mix = _rwkv_layer(h, g6, rwkv_mu[j], rwkv_w_rkv[j], rwkv_w0[j], rwkv_w1[j], rwkv_w2[j],
                              rwkv_a0[j], rwkv_a1[j], rwkv_a2[j], rwkv_g1[j], rwkv_g2[j],
                              rwkv_k_k[j], rwkv_k_a[j], rwkv_r_k[j], rwkv_lnx_g[j], rwkv_lnx_b[j],
                              rwkv_w_o[j], bsz, seq)
        h = _ffn(h, g6, win, wout, layer, 1, mixer=mix)
    return h.reshape(bsz, seq, d)
```

```python
import functools

import jax
import jax.numpy as jnp
from jax import lax
from jax.experimental import pallas as pl
from jax.experimental.pallas import tpu as pltpu

F32 = jnp.float32
BF16 = jnp.bfloat16

NORM_EPS = 1e-6
GLA_NORM_EPS = 1e-5
GLA_GATE_TEMP = 16.0
RWKV_LNX_EPS = 64e-5
LOG2_E = 1.4426950408889634

D_FF_CHUNK = 256
TOKEN_BLOCK = 512
FFN_TOKEN_BLOCK = 1024
FFN_SUB_ROWS = 512
SWA_BLOCK = 128
SWA_Q_BLOCKS = 4
SWA_HEAD_DIM = 64
SWA_HEADS = 16
SWA_KV_HEADS = 4
GLA_HEADS = 4
GLA_CHUNK = 64
GLA_SEQ_BLOCK = 512
RWKV_HEAD = 64
RWKV_CHUNK = 64
RWKV_SEQ_BLOCK = 512
RWKV_PAIRS = 8
RWKV_BATCH_ROWS = 2
LANES = 128
VMEM_LIMIT = 56 * 1024 * 1024


def _params(sem):
    return pltpu.CompilerParams(dimension_semantics=sem, vmem_limit_bytes=VMEM_LIMIT)


def _rmsnorm(x, g, eps=NORM_EPS):
    return x * lax.rsqrt(jnp.mean(x * x, axis=-1, keepdims=True) + eps) * g


def _sigmoid(x):
    return 1.0 / (1.0 + jnp.exp(-x))


def _softplus(x):
    return jnp.maximum(x, 0.0) + jnp.log(1.0 + jnp.exp(-jnp.abs(x)))


def _dot(a, b):
    return jnp.dot(a, b, preferred_element_type=F32)


def _dot_nt(a, b):
    return lax.dot_general(a, b, (((1,), (1,)), ((), ())), preferred_element_type=F32)


def _dot_tn(a, b):
    return lax.dot_general(a, b, (((0,), (0,)), ((), ())), preferred_element_type=F32)


def _full(shape):
    n = len(shape)
    return pl.BlockSpec(shape, lambda *_: (0,) * n)


def _ffn_body(*refs, g_row, mixer_proj):
    if mixer_proj:
        h_ref, a_ref, wo_ref, bo_ref, g_ref, win_ref, wout_ref, o_ref, xn_ref, acc_ref = refs
    else:
        h_ref, g_ref, win_ref, wout_ref, o_ref, xn_ref, acc_ref = refs
    d_ff = wout_ref.shape[0]
    tf = D_FF_CHUNK
    tm = h_ref.shape[0]
    subs = [slice(r0, r0 + FFN_SUB_ROWS) for r0 in range(0, tm, FFN_SUB_ROWS)]
    for rs in subs:
        hs = h_ref[rs, :]
        if mixer_proj:
            m = _dot(a_ref[rs, :], wo_ref[...]) + bo_ref[...]
            hs = hs + _rmsnorm(m, g_ref[g_row - 1:g_row, :])
            o_ref[rs, :] = hs
        xn_ref[rs, :] = _rmsnorm(hs, g_ref[g_row:g_row + 1, :]).astype(BF16)
    res_ref = o_ref if mixer_proj else h_ref
    for rs in subs:
        for c0 in range(0, d_ff, tf):
            xn = xn_ref[rs, :]
            gate = _dot(xn, win_ref[:, c0:c0 + tf])
            up = _dot(xn, win_ref[:, d_ff + c0:d_ff + c0 + tf])
            act = (gate * _sigmoid(gate) * up).astype(BF16)
            part = _dot(act, wout_ref[c0:c0 + tf, :])
            if c0 == 0:
                acc_ref[rs, :] = part
            else:
                acc_ref[rs, :] += part
    for rs in subs:
        o_ref[rs, :] = res_ref[rs, :] + 0.5 * _rmsnorm(acc_ref[rs, :], g_ref[g_row + 1:g_row + 2, :])


def _resident(shape):
    n = len(shape)
    return pl.BlockSpec(shape, lambda *_: (0,) * n, pipeline_mode=pl.Buffered(1))


def _stacked_weight(w, *lead):
    idx = tuple(lead) + (0, 0)
    return pl.BlockSpec((None,) * len(lead) + w.shape[-2:], lambda *_: idx,
                        pipeline_mode=pl.Buffered(1))


def _ffn(h, g6, win, wout, layer, which, mixer=None):
    t, d = h.shape
    g_row = 4 * which
    tm = FFN_TOKEN_BLOCK
    tok = pl.BlockSpec((tm, d), lambda i: (i, 0))
    ins, in_specs = [h], [tok]
    if mixer is not None:
        a, w_o, b_o = mixer
        ins += [a, w_o, b_o]
        in_specs += [pl.BlockSpec((tm, a.shape[1]), lambda i: (i, 0)), _resident(w_o.shape),
                     _full(b_o.shape)]
    ins += [g6, win, wout]
    in_specs += [_full(g6.shape), _stacked_weight(win, layer, which),
                 _stacked_weight(wout, layer, which)]
    return pl.pallas_call(
        functools.partial(_ffn_body, g_row=g_row, mixer_proj=mixer is not None),
        grid=(t // tm,),
        in_specs=in_specs,
        out_specs=tok,
        out_shape=jax.ShapeDtypeStruct((t, d), F32),
        scratch_shapes=[pltpu.VMEM((tm, d), BF16), pltpu.VMEM((tm, d), F32)],
        compiler_params=_params(("parallel",)),
        name="ffn_mix" if mixer is not None else "ffn",
    )(*ins)


def _norm_proj_body(h_ref, g_ref, w_ref, b_ref, *o_refs, g_row, n_chunk):
    xn = _rmsnorm(h_ref[...], g_ref[g_row:g_row + 1, :]).astype(BF16)
    col = 0
    for o_ref in o_refs:
        n = o_ref.shape[1]
        for c in range(0, n, n_chunk):
            cs = slice(col + c, col + min(c + n_chunk, n))
            y = _dot(xn, w_ref[:, cs]) + b_ref[:, cs]
            o_ref[:, c:c + cs.stop - cs.start] = y.astype(o_ref.dtype)
        col += n


def _norm_proj(h, g6, g_row, w, b, widths):
    t, d = h.shape
    tm = TOKEN_BLOCK
    return pl.pallas_call(
        functools.partial(_norm_proj_body, g_row=g_row, n_chunk=512),
        grid=(t // tm,),
        in_specs=[
            pl.BlockSpec((tm, d), lambda i: (i, 0)),
            _full(g6.shape),
            _resident(w.shape),
            _full(b.shape),
        ],
        out_specs=[pl.BlockSpec((tm, n), lambda i: (i, 0)) for n in widths],
        out_shape=[jax.ShapeDtypeStruct((t, n), BF16) for n in widths],
        compiler_params=_params(("parallel",)),
        name="norm_proj",
    )(h, g6, w, b)


def _swa_body(sink_ref, q_ref, kp_ref, kc_ref, vp_ref, vc_ref, o_ref):
    first_step = pl.program_id(1) == 0
    blk = SWA_BLOCK
    nkv = SWA_KV_HEADS
    nq = q_ref.shape[0] // blk
    kall = jnp.concatenate([kp_ref[...], kc_ref[...]], axis=0)
    vall = jnp.concatenate([vp_ref[...], vc_ref[...]], axis=0)
    own = (lax.broadcasted_iota(jnp.int32, (blk, blk), 1)
           <= lax.broadcasted_iota(jnp.int32, (blk, blk), 0))
    no_prev = jnp.where(first_step, -jnp.inf, 0.0)
    width = nkv * SWA_HEAD_DIM
    lane_head = lax.broadcasted_iota(jnp.int32, kall.shape, 1) // SWA_HEAD_DIM
    scale = SWA_HEAD_DIM ** -0.5
    groups = SWA_HEADS // nkv
    kmask = [kall * jnp.where(lane_head == hh, scale, 0.0).astype(BF16) for hh in range(nkv)]
    vmask = [vall * jnp.where(lane_head == hh, 1.0, 0.0).astype(BF16) for hh in range(nkv)]
    work = [(j, s) for j in range(nq) for s in range(groups)]

    def keys_of(masked, j):
        return jnp.concatenate([m[j * blk:(j + 2) * blk] for m in masked], axis=0)

    def softmax(j, s, sc4):
        ps = []
        for hh in range(nkv):
            c0 = hh * 2 * blk
            prev = sc4[:, c0:c0 + blk]
            if j == 0:
                prev = prev + no_prev
            sc = jnp.where(own, sc4[:, c0 + blk:c0 + 2 * blk], prev)
            sink = sink_ref[hh * groups + s]
            m = jnp.maximum(jnp.max(sc, axis=-1, keepdims=True), sink)
            p = jnp.exp(sc - m)
            inv = 1.0 / (jnp.sum(p, axis=-1, keepdims=True) + jnp.exp(sink - m))
            p = p * inv
            ps.append(jnp.where(own, 0.0, p).astype(BF16))
            ps.append(jnp.where(own, p, 0.0).astype(BF16))
        return jnp.concatenate(ps, axis=1)

    kstack = [keys_of(kmask, j) for j in range(nq)]
    vstack = [keys_of(vmask, j) for j in range(nq)]
    scores = [_dot_nt(q_ref[j * blk:(j + 1) * blk, s * width:(s + 1) * width], kstack[j])
              for j, s in work]
    probs = [softmax(j, s, scores[i]) for i, (j, s) in enumerate(work)]
    for i, (j, s) in enumerate(work):
        o_ref[j * blk:(j + 1) * blk, s * width:(s + 1) * width] = (
            _dot(probs[i], vstack[j]).astype(o_ref.dtype))


def _swa_core(q, k, v, sinks, bsz, seq):
    blk = SWA_BLOCK
    nq = SWA_Q_BLOCKS
    ns = seq // (nq * blk)
    dq, dkv = q.shape[1], k.shape[1]
    cur = lambda b, n: (b * ns + n, 0)
    prev = lambda b, n: ((b * ns + n) * nq - jnp.minimum(n, 1), 0)
    return pl.pallas_call(
        _swa_body,
        grid=(bsz, ns),
        in_specs=[
            pl.BlockSpec(memory_space=pltpu.SMEM),
            pl.BlockSpec((nq * blk, dq), cur),
            pl.BlockSpec((blk, dkv), prev),
            pl.BlockSpec((nq * blk, dkv), cur),
            pl.BlockSpec((blk, dkv), prev),
            pl.BlockSpec((nq * blk, dkv), cur),
        ],
        out_specs=pl.BlockSpec((nq * blk, dq), cur),
        out_shape=jax.ShapeDtypeStruct((bsz * seq, dq), BF16),
        compiler_params=_params(("parallel", "parallel")),
        name="swa_core",
    )(sinks, q, k, k, v, v)


def _swa_layer(h, g6, w_qkv, b_qkv, sinks, w_o, b_o, bsz, seq):
    d = h.shape[1]
    hd, nh, nkv = SWA_HEAD_DIM, SWA_HEADS, SWA_KV_HEADS
    grp = nh // nkv
    dq = nh * hd
    wq = w_qkv[:, :dq].reshape(d, nkv, grp, hd).transpose(0, 2, 1, 3).reshape(d, dq)
    bq = b_qkv[:dq].reshape(nkv, grp, hd).transpose(1, 0, 2).reshape(dq)
    w = jnp.concatenate([wq, w_qkv[:, dq:]], axis=1).astype(BF16)
    b = jnp.concatenate([bq, b_qkv[dq:]])[None, :]
    wo = w_o.reshape(nkv, grp, hd, d).transpose(1, 0, 2, 3).reshape(dq, d).astype(BF16)
    dkv = nkv * hd
    q, k, v = _norm_proj(h, g6, 2, w, b, (dq, dkv, dkv))
    att = _swa_core(q, k, v, sinks, bsz, seq)
    return att, wo, b_o[None, :]


def _split3(x):
    hi = x.astype(BF16)
    r1 = x - hi.astype(F32)
    mid = r1.astype(BF16)
    lo = (r1 - mid.astype(F32)).astype(BF16)
    return hi, mid, lo


def _gla_in_body(h_ref, g_ref, w_ref, wl_ref, w2_ref, bg_ref, y_ref, b_ref, *, n_chunk):
    tm = h_ref.shape[0]
    c = GLA_CHUNK
    xn = _rmsnorm(h_ref[...], g_ref[2:3, :]).astype(BF16)
    for c0 in range(0, w_ref.shape[1], n_chunk):
        y_ref[:, c0:c0 + n_chunk] = _dot(xn, w_ref[:, c0:c0 + n_chunk]).astype(y_ref.dtype)
    g_low = _dot(xn, wl_ref[...]).astype(BF16)
    z = _dot(g_low, w2_ref[...]) + bg_ref[...]
    la = -_softplus(-z) * (LOG2_E / GLA_GATE_TEMP)
    tri = jnp.where(lax.broadcasted_iota(jnp.int32, (c, c), 0)
                    >= lax.broadcasted_iota(jnp.int32, (c, c), 1), 1.0, 0.0).astype(BF16)
    for r0 in range(0, tm, c):
        hi, mid, lo = _split3(la[r0:r0 + c, :])
        b_ref[r0:r0 + c, :] = _dot(tri, hi) + _dot(tri, mid) + _dot(tri, lo)


def _gla_in(h, g6, w_main, w_low, w_gate2, b_gate):
    t, d = h.shape
    n = w_main.shape[1]
    dk = w_gate2.shape[1]
    tm = TOKEN_BLOCK
    return pl.pallas_call(
        functools.partial(_gla_in_body, n_chunk=512),
        grid=(t // tm,),
        in_specs=[
            pl.BlockSpec((tm, d), lambda i: (i, 0)),
            _full(g6.shape),
            _resident(w_main.shape),
            _resident(w_low.shape),
            _resident(w_gate2.shape),
            _full(b_gate.shape),
        ],
        out_specs=[pl.BlockSpec((tm, n), lambda i: (i, 0)), pl.BlockSpec((tm, dk), lambda i: (i, 0))],
        out_shape=[jax.ShapeDtypeStruct((t, n), BF16), jax.ShapeDtypeStruct((t, dk), F32)],
        compiler_params=_params(("parallel",)),
        name="gla_in",
    )(h, g6, w_main, w_low, w_gate2, b_gate)


def _block_first(x, s):
    c, l = x.shape
    xb = x.reshape(c // s, s, l)
    return jnp.broadcast_to(xb[:, 0:1, :], (c // s, s, l)).reshape(c, l)


def _gla_body(q_ref, k_ref, v_ref, go_ref, b_ref, ng_ref, o_ref, st_ref, *, n_chunks, n_heads):
    c = GLA_CHUNK
    dk = q_ref.shape[1] // n_heads
    dv = v_ref.shape[1] // n_heads
    heads = range(n_heads)

    @pl.when(pl.program_id(1) == 0)
    def _():
        st_ref[...] = jnp.zeros_like(st_ref)

    row = lax.broadcasted_iota(jnp.int32, (c, 1), 0)
    ri = lax.broadcasted_iota(jnp.int32, (c, c), 0)
    ci = lax.broadcasted_iota(jnp.int32, (c, c), 1)
    levels = (8, 16, 32)
    odd = [(row // s) % 2 == 1 for s in levels]
    same_group = [ri // (2 * s) == ci // (2 * s) for s in levels]
    hit = [(ci == 8 * (ri // 8) + jj) & (ri % 8 >= jj) for jj in range(8)]

    def chunk(ic, carry):
        rows = pl.ds(pl.multiple_of(ic * c, c), c)
        q, k, v, b = [], [], [], []
        for h in heads:
            q.append(q_ref[rows, h * dk:(h + 1) * dk].astype(F32) * (dk ** -0.5))
            k.append(k_ref[rows, h * dk:(h + 1) * dk].astype(F32))
            v.append(v_ref[rows, h * dv:(h + 1) * dv])
            b.append(b_ref[rows, h * dk:(h + 1) * dk])
        st = [st_ref[h] for h in heads]
        o = [_dot_nt((q[h] * jnp.exp2(b[h])).astype(BF16), st[h].astype(BF16)) for h in heads]
        for h in heads:
            b_last = b[h][c - 1:c, :]
            kd = (k[h] * jnp.exp2(b_last - b[h])).astype(BF16)
            st_ref[h] = st[h] * jnp.exp2(b_last) + _dot_tn(v[h], kd)
        al = []
        for h in heads:
            for i, s in enumerate(levels):
                first = _block_first(b[h], s)
                nxt = jnp.concatenate([first[s:], first[:s]], axis=0)
                qh = jnp.where(odd[i], q[h] * jnp.exp2(b[h] - first), 0.0)
                kh = jnp.where(odd[i], 0.0, k[h] * jnp.exp2(jnp.minimum(nxt - b[h], 0.0)))
                al.append(_dot_nt(qh.astype(BF16), kh.astype(BF16)))
        a = []
        for h in heads:
            ah = jnp.zeros((c, c), F32)
            for i in range(len(levels)):
                ah = ah + jnp.where(same_group[i], al[h * len(levels) + i], 0.0)
            k3 = k[h].reshape(c // 8, 8, dk)
            b3 = b[h].reshape(c // 8, 8, dk)
            for jj in range(8):
                kj = jnp.broadcast_to(k3[:, jj:jj + 1, :], k3.shape).reshape(c, dk)
                bj = jnp.broadcast_to(b3[:, jj:jj + 1, :], b3.shape).reshape(c, dk)
                tt = q[h] * kj * jnp.exp2(jnp.minimum(b[h] - bj, 0.0))
                ah = jnp.where(hit[jj], jnp.sum(tt, axis=-1, keepdims=True), ah)
            a.append(ah.astype(BF16))
        o = [o[h] + _dot(a[h], v[h]) for h in heads]
        for h in heads:
            on = _rmsnorm(o[h], ng_ref[...], GLA_NORM_EPS)
            go = go_ref[rows, h * dv:(h + 1) * dv].astype(F32)
            o_ref[rows, h * dv:(h + 1) * dv] = (on * (go * _sigmoid(go))).astype(o_ref.dtype)
        return carry

    lax.fori_loop(0, n_chunks, chunk, 0, unroll=2)


def _gla_core(y, bcum, norm_g, bsz, seq):
    nh = GLA_HEADS
    dqk = bcum.shape[1]
    dvv = (y.shape[1] - 2 * dqk) // 2
    ts = GLA_SEQ_BLOCK
    ns = seq // ts
    rowblk = lambda b, s: b * ns + s
    return pl.pallas_call(
        functools.partial(_gla_body, n_chunks=ts // GLA_CHUNK, n_heads=nh),
        grid=(bsz, ns),
        in_specs=[
            pl.BlockSpec((ts, dqk), lambda b, s: (rowblk(b, s), 0)),
            pl.BlockSpec((ts, dqk), lambda b, s: (rowblk(b, s), 1)),
            pl.BlockSpec((ts, dvv), lambda b, s: (rowblk(b, s), (2 * dqk) // dvv)),
            pl.BlockSpec((ts, dvv), lambda b, s: (rowblk(b, s), (2 * dqk) // dvv + 1)),
            pl.BlockSpec((ts, dqk), lambda b, s: (rowblk(b, s), 0)),
            _full(norm_g.shape),
        ],
        out_specs=pl.BlockSpec((ts, dvv), lambda b, s: (rowblk(b, s), 0)),
        out_shape=jax.ShapeDtypeStruct((bsz * seq, dvv), BF16),
        scratch_shapes=[pltpu.VMEM((nh, dvv // nh, dqk // nh), F32)],
        compiler_params=_params(("parallel", "arbitrary")),
        name="gla_core",
    )(y, y, y, y, bcum, norm_g)


def _gla_layer(h, g6, w_in, w_gate2, b_gate, norm_g, w_o, bsz, seq):
    d = h.shape[1]
    rank = w_gate2.shape[0]
    n_main = w_in.shape[1] - rank
    w_main = w_in[:, :n_main].astype(BF16)
    w_low = jnp.pad(w_in[:, n_main:], ((0, 0), (0, LANES - rank))).astype(BF16)
    w2 = jnp.pad(w_gate2, ((0, LANES - rank), (0, 0))).astype(BF16)
    y, bcum = _gla_in(h, g6, w_main, w_low, w2, b_gate[None, :])
    o = _gla_core(y, bcum, norm_g[None, :], bsz, seq)
    return o, w_o.astype(BF16), jnp.zeros((1, d), F32)


def _rwkv_pre_body(h_ref, hp_ref, g_ref, mu_ref, wrkv_ref, w1_ref, w2_ref, a1_ref, a2_ref,
                   g1_ref, g2_ref, vec_ref, r_ref, k_ref, v_ref, lw_ref, na_ref, nb_ref, gt_ref,
                   *, blocks_per_seq):
    i = pl.program_id(0)
    tm, d = h_ref.shape
    gain = g_ref[2:3, :]
    x = _rmsnorm(h_ref[...], gain)
    prev = _rmsnorm(hp_ref[...], gain)[7:8, :]
    prev = jnp.where(i % blocks_per_seq == 0, 0.0, prev)
    row = lax.broadcasted_iota(jnp.int32, (tm, 1), 0)
    xs = jnp.where(row == 0, prev, pltpu.roll(x, shift=1, axis=0))
    xx = xs - x
    li = lax.broadcasted_iota(jnp.int32, (LANES, LANES), 0) // RWKV_HEAD
    lj = lax.broadcasted_iota(jnp.int32, (LANES, LANES), 1) // RWKV_HEAD
    gmat = jnp.where(li == lj, 1.0, 0.0).astype(BF16)

    def mix(j):
        return (x + xx * mu_ref[j:j + 1, :]).astype(BF16)

    r = _dot(mix(0), wrkv_ref[0])
    k = _dot(mix(2), wrkv_ref[1])
    v = _dot(mix(3), wrkv_ref[2])
    zw = vec_ref[0:1, :] + _dot(jnp.tanh(_dot(mix(1), w1_ref[...])).astype(BF16), w2_ref[...])
    lw = -LOG2_E * jnp.exp(-_softplus(-zw) - 0.5)
    a = _sigmoid(vec_ref[1:2, :] + _dot(_dot(mix(4), a1_ref[...]).astype(BF16), a2_ref[...]))
    gt = _dot(_sigmoid(_dot(mix(5), g1_ref[...])).astype(BF16), g2_ref[...])
    kk = k * vec_ref[2:3, :]
    r_ref[...] = r.astype(r_ref.dtype)
    v_ref[...] = v.astype(v_ref.dtype)
    lw_ref[...] = lw
    gt_ref[...] = gt.astype(gt_ref.dtype)
    k_ref[...] = (k * (1.0 + (a - 1.0) * vec_ref[3:4, :])).astype(k_ref.dtype)
    for s in range(0, d, LANES):
        kks = kk[:, s:s + LANES]
        ss = _dot((kks * kks).astype(BF16), gmat)
        kkn = kks / jnp.maximum(jnp.sqrt(ss), 1e-12)
        na_ref[:, s:s + LANES] = (-kkn).astype(na_ref.dtype)
        nb_ref[:, s:s + LANES] = (kkn * a[:, s:s + LANES]).astype(nb_ref.dtype)


def _rwkv_pre(h, g6, mu, wrkv, w1, w2, a1, a2, g1, g2, vec, seq):
    t, d = h.shape
    tm = TOKEN_BLOCK
    tok = pl.BlockSpec((tm, d), lambda i: (i, 0))
    ins = [h, h, g6, mu, wrkv, w1, w2, a1, a2, g1, g2, vec]
    in_specs = [tok, pl.BlockSpec((8, d), lambda i: (jnp.maximum(i * (tm // 8) - 1, 0), 0))]
    in_specs += [_full(x.shape) for x in ins[2:]]
    return pl.pallas_call(
        functools.partial(_rwkv_pre_body, blocks_per_seq=seq // tm),
        grid=(t // tm,),
        in_specs=in_specs,
        out_specs=[tok] * 7,
        out_shape=[jax.ShapeDtypeStruct((t, d), F32 if i == 3 else BF16) for i in range(7)],
        compiler_params=_params(("parallel",)),
        name="rwkv_pre",
    )(*ins)


def _rwkv_body(r_ref, k_ref, v_ref, lw_ref, na_ref, nb_ref, gt_ref, vec_ref, o_ref, s_ref,
               *, n_chunks, n_pairs):
    c = RWKV_CHUNK
    hd = RWKV_HEAD
    n_rows = r_ref.shape[0]
    pairs = range(n_rows * n_pairs)

    def where(p):
        return p // n_pairs, slice((p % n_pairs) * LANES, (p % n_pairs + 1) * LANES)

    @pl.when(pl.program_id(2) == 0)
    def _():
        s_ref[...] = jnp.zeros_like(s_ref)

    lane = lax.broadcasted_iota(jnp.int32, (c, LANES), 1)
    head0 = lane < hd
    ri = lax.broadcasted_iota(jnp.int32, (2 * c, 2 * c), 0)
    ci = lax.broadcasted_iota(jnp.int32, (2 * c, 2 * c), 1)
    strict = ri > ci
    incl = ri >= ci
    tri = jnp.where(lax.broadcasted_iota(jnp.int32, (c, c), 0)
                    >= lax.broadcasted_iota(jnp.int32, (c, c), 1), 1.0, 0.0).astype(BF16)

    def stack(x):
        return jnp.concatenate([jnp.where(head0, x, 0.0), jnp.where(head0, 0.0, x)], axis=0)

    def hsum(x):
        s0 = jnp.sum(jnp.where(head0, x, 0.0), axis=-1, keepdims=True)
        s1 = jnp.sum(jnp.where(head0, 0.0, x), axis=-1, keepdims=True)
        return jnp.where(head0, s0, s1)

    def chunk(ic, carry):
        r0 = pl.multiple_of(ic * c, c)
        rows = pl.ds(r0, c)
        cl_all = []
        for bb in range(n_rows):
            hi, mid, lo = _split3(lw_ref[bb, rows, :])
            cl_all.append(_dot(tri, hi) + _dot(tri, mid) + _dot(tri, lo))
        lhs, lhs_flat, rhs, rhs_end, vsb, sb, decay = [], [], [], [], [], [], []
        for p in pairs:
            bb, sl = where(p)
            r, k, v = (x[bb, rows, sl].astype(F32) for x in (r_ref, k_ref, v_ref))
            lw = lw_ref[bb, rows, sl]
            na, nb = na_ref[bb, rows, sl].astype(F32), nb_ref[bb, rows, sl].astype(F32)
            cl = cl_all[bb][:, sl]
            cl_last = cl[c - 1:c, :]
            p_inv = jnp.exp2(-cl)
            p_end = jnp.exp2(cl_last - cl)
            at, rt = na * jnp.exp2(cl - lw), r * jnp.exp2(cl)
            lhs.append(jnp.concatenate([stack(at), stack(rt)], axis=0).astype(BF16))
            lhs_flat.append(jnp.concatenate([at, rt], axis=0).astype(BF16))
            rhs.append(jnp.concatenate([stack(nb * p_inv), stack(k * p_inv)], axis=0).astype(BF16))
            rhs_end.append(jnp.concatenate([stack(nb * p_end), stack(k * p_end)], axis=0).astype(BF16))
            vsb.append(stack(v).astype(BF16))
            sb.append(s_ref[p].astype(BF16))
            decay.append(jnp.exp2(cl_last))
        gram = [_dot_nt(lhs[p], rhs[p]) for p in pairs]
        ws = [_dot_nt(lhs_flat[p], sb[p]) for p in pairs]
        a_ab = [jnp.where(strict, gram[p][:2 * c, :2 * c], 0.0) for p in pairs]
        a_ak = [jnp.where(strict, gram[p][:2 * c, 2 * c:], 0.0).astype(BF16) for p in pairs]
        a_r = [jnp.concatenate([jnp.where(incl, gram[p][2 * c:, :2 * c], 0.0),
                                jnp.where(incl, gram[p][2 * c:, 2 * c:], 0.0)], axis=1).astype(BF16)
               for p in pairs]
        u = [stack(ws[p][:c]) + _dot(a_ak[p], vsb[p]) for p in pairs]
        apow = a_ab
        steps = (c - 1).bit_length()
        for it in range(steps):
            ab = [apow[p].astype(BF16) for p in pairs]
            if it + 1 < steps:
                both = [_dot(ab[p], jnp.concatenate([u[p].astype(BF16), ab[p]], axis=1))
                        for p in pairs]
                u = [u[p] + both[p][:, :LANES] for p in pairs]
                apow = [both[p][:, LANES:] for p in pairs]
            else:
                u = [u[p] + _dot(ab[p], u[p].astype(BF16)) for p in pairs]
        uv = [jnp.concatenate([u[p].astype(BF16), vsb[p]], axis=0) for p in pairs]
        ys = [_dot(a_r[p], uv[p]) for p in pairs]
        for p in pairs:
            s_ref[p] = s_ref[p] * decay[p] + _dot_tn(uv[p], rhs_end[p])
        for p in pairs:
            bb, sl = where(p)
            y = ys[p][:c, :] + ys[p][c:, :] + ws[p][c:, :]
            mean = hsum(y) * (1.0 / hd)
            yc = y - mean
            var = hsum(yc * yc) * (1.0 / hd)
            yn = yc * lax.rsqrt(var + RWKV_LNX_EPS) * vec_ref[1:2, sl] + vec_ref[2:3, sl]
            r, k, v = (x[bb, rows, sl].astype(F32) for x in (r_ref, k_ref, v_ref))
            yn = yn + hsum(r * k * vec_ref[0:1, sl]) * v
            o_ref[bb, rows, sl] = (yn * gt_ref[bb, rows, sl].astype(F32)).astype(o_ref.dtype)
        return carry

    lax.fori_loop(0, n_chunks, chunk, 0, unroll=2)


def _rwkv_core(r, k, v, lw, na, nb, gt, vec, bsz, seq):
    t, d = r.shape
    ts = RWKV_SEQ_BLOCK
    nr = RWKV_BATCH_ROWS
    width = RWKV_PAIRS * LANES
    tok = pl.BlockSpec((nr, ts, width), lambda b, p, s: (b, s, p))
    seqs = [x.reshape(bsz, seq, d) for x in (r, k, v, lw, na, nb, gt)]
    out = pl.pallas_call(
        functools.partial(_rwkv_body, n_chunks=ts // RWKV_CHUNK, n_pairs=RWKV_PAIRS),
        grid=(bsz // nr, d // width, seq // ts),
        in_specs=[tok] * 7 + [pl.BlockSpec((vec.shape[0], width), lambda b, p, s: (0, p))],
        out_specs=tok,
        out_shape=jax.ShapeDtypeStruct((bsz, seq, d), BF16),
        scratch_shapes=[pltpu.VMEM((nr * RWKV_PAIRS, LANES, LANES), F32)],
        compiler_params=_params(("parallel", "parallel", "arbitrary")),
        name="rwkv_core",
    )(*seqs, vec)
    return out.reshape(t, d)


def _pad_to(x, axis, size):
    pads = [(0, 0)] * x.ndim
    pads[axis] = (0, size - x.shape[axis])
    return jnp.pad(x, pads)


def _rwkv_layer(h, g6, mu, w_rkv, w0, w1, w2, a0, a1, a2, g1, g2, k_k, k_a, r_k, lnx_g, lnx_b,
                w_o, bsz, seq):
    d = h.shape[1]
    up = lambda n: -(-n // LANES) * LANES
    w1p = _pad_to(w1, 1, up(w1.shape[1])).astype(BF16)
    w2p = _pad_to(w2, 0, up(w2.shape[0])).astype(BF16)
    a1p = _pad_to(a1, 1, up(a1.shape[1])).astype(BF16)
    a2p = _pad_to(a2, 0, up(a2.shape[0])).astype(BF16)
    g1p = _pad_to(g1, 1, up(g1.shape[1])).astype(BF16)
    g2p = _pad_to(g2, 0, up(g2.shape[0])).astype(BF16)
    vec_pre = _pad_to(jnp.stack([w0, a0, k_k, k_a]), 0, 8)
    outs = _rwkv_pre(h, g6, _pad_to(mu, 0, 8), w_rkv.astype(BF16), w1p, w2p, a1p, a2p, g1p, g2p,
                     vec_pre, seq)
    vec_core = _pad_to(jnp.stack([r_k.reshape(d), lnx_g, lnx_b]), 0, 8)
    o = _rwkv_core(*outs, vec_core, bsz, seq)
    return o, w_o.astype(BF16), jnp.zeros((1, d), F32)


def kernel(x, norm_g, ffn_w_in, ffn_w_out, swa_w_qkv, swa_b_qkv, swa_sinks, swa_w_o, swa_b_o, gla_w_in, gla_w_gate2, gla_b_gate, gla_norm_g, gla_w_o, rwkv_mu, rwkv_w_rkv, rwkv_w0, rwkv_w1, rwkv_w2, rwkv_a0, rwkv_a1, rwkv_a2, rwkv_g1, rwkv_g2, rwkv_k_k, rwkv_k_a, rwkv_r_k, rwkv_lnx_g, rwkv_lnx_b, rwkv_w_o):
    bsz, seq, d = x.shape
    depth = norm_g.shape[0]
    h = x.reshape(bsz * seq, d)
    win, wout = ffn_w_in.astype(BF16), ffn_w_out.astype(BF16)
    for layer in range(depth):
        g6 = norm_g[layer]
        h = _ffn(h, g6, win, wout, layer, 0)
        kind, j = layer % 3, layer // 3
        if kind == 0:
            mix = _swa_layer(h, g6, swa_w_qkv[j], swa_b_qkv[j], swa_sinks[j], swa_w_o[j],
                             swa_b_o[j], bsz, seq)
        elif kind == 1:
            mix = _gla_layer(h, g6, gla_w_in[j], gla_w_gate2[j], gla_b_gate[j], gla_norm_g[j],
                             gla_w_o[j], bsz, seq)
        else:
            mix = _rwkv_layer(h, g6, rwkv_mu[j], rwkv_w_rkv[j], rwkv_w0[j], rwkv_w1[j], rwkv_w2[j],
                              rwkv_a0[j], rwkv_a1[j], rwkv_a2[j], rwkv_g1[j], rwkv_g2[j],
                              rwkv_k_k[j], rwkv_k_a[j], rwkv_r_k[j], rwkv_lnx_g[j], rwkv_lnx_b[j],
                              rwkv_w_o[j], bsz, seq)
        h = _ffn(h, g6, win, wout, layer, 1, mixer=mix)
    return h.reshape(bsz, seq, d)
```

```python
import functools

import jax
import jax.numpy as jnp
from jax import lax
from jax.experimental import pallas as pl
from jax.experimental.pallas import tpu as pltpu

F32 = jnp.float32
BF16 = jnp.bfloat16

NORM_EPS = 1e-6
GLA_NORM_EPS = 1e-5
GLA_GATE_TEMP = 16.0
RWKV_LNX_EPS = 64e-5
LOG2_E = 1.4426950408889634

D_FF_CHUNK = 256
TOKEN_BLOCK = 512
PROJ_TOKEN_BLOCK = 1024
PROJ_SUB_ROWS = 512
FFN_TOKEN_BLOCK = 1024
FFN_SUB_ROWS = 512
SWA_BLOCK = 128
SWA_Q_BLOCKS = 8
SWA_HEAD_DIM = 64
SWA_HEADS = 16
SWA_KV_HEADS = 4
GLA_HEADS = 4
GLA_CHUNK = 64
GLA_SEQ_BLOCK = 512
RWKV_HEAD = 64
RWKV_CHUNK = 64
RWKV_SEQ_BLOCK = 512
RWKV_PAIRS = 8
RWKV_BATCH_ROWS = 2
LANES = 128
VMEM_LIMIT = 56 * 1024 * 1024


def _params(sem):
    return pltpu.CompilerParams(dimension_semantics=sem, vmem_limit_bytes=VMEM_LIMIT)


def _rmsnorm(x, g, eps=NORM_EPS):
    return x * lax.rsqrt(jnp.mean(x * x, axis=-1, keepdims=True) + eps) * g


def _sigmoid(x):
    return 1.0 / (1.0 + jnp.exp(-x))


def _softplus(x):
    return jnp.maximum(x, 0.0) + jnp.log(1.0 + jnp.exp(-jnp.abs(x)))


def _dot(a, b):
    return jnp.dot(a, b, preferred_element_type=F32)


def _dot_nt(a, b):
    return lax.dot_general(a, b, (((1,), (1,)), ((), ())), preferred_element_type=F32)


def _dot_tn(a, b):
    return lax.dot_general(a, b, (((0,), (0,)), ((), ())), preferred_element_type=F32)


def _full(shape):
    n = len(shape)
    return pl.BlockSpec(shape, lambda *_: (0,) * n)


def _ffn_body(*refs, g_row, mixer_proj):
    if mixer_proj:
        h_ref, a_ref, wo_ref, bo_ref, g_ref, win_ref, wout_ref, o_ref, xn_ref, acc_ref = refs
    else:
        h_ref, g_ref, win_ref, wout_ref, o_ref, xn_ref, acc_ref = refs
    d_ff = wout_ref.shape[0]
    tf = D_FF_CHUNK
    tm = h_ref.shape[0]
    subs = [slice(r0, r0 + FFN_SUB_ROWS) for r0 in range(0, tm, FFN_SUB_ROWS)]
    for rs in subs:
        hs = h_ref[rs, :]
        if mixer_proj:
            m = _dot(a_ref[rs, :], wo_ref[...]) + bo_ref[...]
            hs = hs + _rmsnorm(m, g_ref[g_row - 1:g_row, :])
            o_ref[rs, :] = hs
        xn_ref[rs, :] = _rmsnorm(hs, g_ref[g_row:g_row + 1, :]).astype(BF16)
    res_ref = o_ref if mixer_proj else h_ref
    for rs in subs:
        for c0 in range(0, d_ff, tf):
            xn = xn_ref[rs, :]
            gate = _dot(xn, win_ref[:, c0:c0 + tf])
            up = _dot(xn, win_ref[:, d_ff + c0:d_ff + c0 + tf])
            act = (gate * _sigmoid(gate) * up).astype(BF16)
            part = _dot(act, wout_ref[c0:c0 + tf, :])
            if c0 == 0:
                acc_ref[rs, :] = part
            else:
                acc_ref[rs, :] += part
    for rs in subs:
        o_ref[rs, :] = res_ref[rs, :] + 0.5 * _rmsnorm(acc_ref[rs, :], g_ref[g_row + 1:g_row + 2, :])


def _resident(shape):
    n = len(shape)
    return pl.BlockSpec(shape, lambda *_: (0,) * n, pipeline_mode=pl.Buffered(1))


def _stacked_weight(w, *lead):
    idx = tuple(lead) + (0, 0)
    return pl.BlockSpec((None,) * len(lead) + w.shape[-2:], lambda *_: idx,
                        pipeline_mode=pl.Buffered(1))


def _ffn(h, g6, win, wout, layer, which, mixer=None):
    t, d = h.shape
    g_row = 4 * which
    tm = FFN_TOKEN_BLOCK
    tok = pl.BlockSpec((tm, d), lambda i: (i, 0))
    ins, in_specs = [h], [tok]
    if mixer is not None:
        a, w_o, b_o = mixer
        ins += [a, w_o, b_o]
        in_specs += [pl.BlockSpec((tm, a.shape[1]), lambda i: (i, 0)), _resident(w_o.shape),
                     _full(b_o.shape)]
    ins += [g6, win, wout]
    in_specs += [_full(g6.shape), _stacked_weight(win, layer, which),
                 _stacked_weight(wout, layer, which)]
    return pl.pallas_call(
        functools.partial(_ffn_body, g_row=g_row, mixer_proj=mixer is not None),
        grid=(t // tm,),
        in_specs=in_specs,
        out_specs=tok,
        out_shape=jax.ShapeDtypeStruct((t, d), F32),
        scratch_shapes=[pltpu.VMEM((tm, d), BF16), pltpu.VMEM((tm, d), F32)],
        compiler_params=_params(("parallel",)),
        name="ffn_mix" if mixer is not None else "ffn",
    )(*ins)


def _row_subs(tm):
    return [slice(r0, r0 + PROJ_SUB_ROWS) for r0 in range(0, tm, PROJ_SUB_ROWS)]


def _norm_proj_body(h_ref, g_ref, w_ref, b_ref, *o_refs, g_row, n_chunk):
    subs = _row_subs(h_ref.shape[0])
    xns = [_rmsnorm(h_ref[rs, :], g_ref[g_row:g_row + 1, :]).astype(BF16) for rs in subs]
    for xn, rs in zip(xns, subs):
        col = 0
        for o_ref in o_refs:
            n = o_ref.shape[1]
            for c in range(0, n, n_chunk):
                cs = slice(col + c, col + min(c + n_chunk, n))
                y = _dot(xn, w_ref[:, cs]) + b_ref[:, cs]
                o_ref[rs, c:c + cs.stop - cs.start] = y.astype(o_ref.dtype)
            col += n


def _norm_proj(h, g6, g_row, w, b, widths):
    t, d = h.shape
    tm = PROJ_TOKEN_BLOCK
    return pl.pallas_call(
        functools.partial(_norm_proj_body, g_row=g_row, n_chunk=512),
        grid=(t // tm,),
        in_specs=[
            pl.BlockSpec((tm, d), lambda i: (i, 0)),
            _full(g6.shape),
            _resident(w.shape),
            _full(b.shape),
        ],
        out_specs=[pl.BlockSpec((tm, n), lambda i: (i, 0)) for n in widths],
        out_shape=[jax.ShapeDtypeStruct((t, n), BF16) for n in widths],
        compiler_params=_params(("parallel",)),
        name="norm_proj",
    )(h, g6, w, b)


def _swa_body(sink_ref, q_ref, kp_ref, kc_ref, vp_ref, vc_ref, o_ref):
    first_step = pl.program_id(1) == 0
    blk = SWA_BLOCK
    nkv = SWA_KV_HEADS
    nq = q_ref.shape[0] // blk
    kall = jnp.concatenate([kp_ref[...], kc_ref[...]], axis=0)
    vall = jnp.concatenate([vp_ref[...], vc_ref[...]], axis=0)
    own = (lax.broadcasted_iota(jnp.int32, (blk, blk), 1)
           <= lax.broadcasted_iota(jnp.int32, (blk, blk), 0))
    no_prev = jnp.where(first_step, -jnp.inf, 0.0)
    width = nkv * SWA_HEAD_DIM
    lane_head = lax.broadcasted_iota(jnp.int32, kall.shape, 1) // SWA_HEAD_DIM
    scale = SWA_HEAD_DIM ** -0.5
    groups = SWA_HEADS // nkv
    kmask = [kall * jnp.where(lane_head == hh, scale, 0.0).astype(BF16) for hh in range(nkv)]
    vmask = [vall * jnp.where(lane_head == hh, 1.0, 0.0).astype(BF16) for hh in range(nkv)]
    work = [(j, s) for j in range(nq) for s in range(groups)]

    def keys_of(masked, j):
        return jnp.concatenate([m[j * blk:(j + 2) * blk] for m in masked], axis=0)

    def softmax(j, s, sc4):
        ps = []
        for hh in range(nkv):
            c0 = hh * 2 * blk
            prev = sc4[:, c0:c0 + blk]
            if j == 0:
                prev = prev + no_prev
            sc = jnp.where(own, sc4[:, c0 + blk:c0 + 2 * blk], prev)
            sink = sink_ref[hh * groups + s]
            m = jnp.maximum(jnp.max(sc, axis=-1, keepdims=True), sink)
            p = jnp.exp(sc - m)
            inv = 1.0 / (jnp.sum(p, axis=-1, keepdims=True) + jnp.exp(sink - m))
            p = p * inv
            ps.append(jnp.where(own, 0.0, p).astype(BF16))
            ps.append(jnp.where(own, p, 0.0).astype(BF16))
        return jnp.concatenate(ps, axis=1)

    kstack = [keys_of(kmask, j) for j in range(nq)]
    vstack = [keys_of(vmask, j) for j in range(nq)]
    scores = [_dot_nt(q_ref[j * blk:(j + 1) * blk, s * width:(s + 1) * width], kstack[j])
              for j, s in work]
    probs = [softmax(j, s, scores[i]) for i, (j, s) in enumerate(work)]
    for i, (j, s) in enumerate(work):
        o_ref[j * blk:(j + 1) * blk, s * width:(s + 1) * width] = (
            _dot(probs[i], vstack[j]).astype(o_ref.dtype))


def _swa_core(q, k, v, sinks, bsz, seq):
    blk = SWA_BLOCK
    nq = SWA_Q_BLOCKS
    ns = seq // (nq * blk)
    dq, dkv = q.shape[1], k.shape[1]
    cur = lambda b, n: (b * ns + n, 0)
    prev = lambda b, n: ((b * ns + n) * nq - jnp.minimum(n, 1), 0)
    return pl.pallas_call(
        _swa_body,
        grid=(bsz, ns),
        in_specs=[
            pl.BlockSpec(memory_space=pltpu.SMEM),
            pl.BlockSpec((nq * blk, dq), cur),
            pl.BlockSpec((blk, dkv), prev),
            pl.BlockSpec((nq * blk, dkv), cur),
            pl.BlockSpec((blk, dkv), prev),
            pl.BlockSpec((nq * blk, dkv), cur),
        ],
        out_specs=pl.BlockSpec((nq * blk, dq), cur),
        out_shape=jax.ShapeDtypeStruct((bsz * seq, dq), BF16),
        compiler_params=_params(("parallel", "parallel")),
        name="swa_core",
    )(sinks, q, k, k, v, v)


def _swa_layer(h, g6, w_qkv, b_qkv, sinks, w_o, b_o, bsz, seq):
    d = h.shape[1]
    hd, nh, nkv = SWA_HEAD_DIM, SWA_HEADS, SWA_KV_HEADS
    grp = nh // nkv
    dq = nh * hd
    wq = w_qkv[:, :dq].reshape(d, nkv, grp, hd).transpose(0, 2, 1, 3).reshape(d, dq)
    bq = b_qkv[:dq].reshape(nkv, grp, hd).transpose(1, 0, 2).reshape(dq)
    w = jnp.concatenate([wq, w_qkv[:, dq:]], axis=1).astype(BF16)
    b = jnp.concatenate([bq, b_qkv[dq:]])[None, :]
    wo = w_o.reshape(nkv, grp, hd, d).transpose(1, 0, 2, 3).reshape(dq, d).astype(BF16)
    dkv = nkv * hd
    q, k, v = _norm_proj(h, g6, 2, w, b, (dq, dkv, dkv))
    att = _swa_core(q, k, v, sinks, bsz, seq)
    return att, wo, b_o[None, :]


def _split3(x):
    hi = x.astype(BF16)
    r1 = x - hi.astype(F32)
    mid = r1.astype(BF16)
    lo = (r1 - mid.astype(F32)).astype(BF16)
    return hi, mid, lo


def _gla_in_body(h_ref, g_ref, w_ref, wl_ref, w2_ref, bg_ref, y_ref, b_ref, *, n_chunk):
    c = GLA_CHUNK
    subs = _row_subs(h_ref.shape[0])
    tri = jnp.where(lax.broadcasted_iota(jnp.int32, (c, c), 0)
                    >= lax.broadcasted_iota(jnp.int32, (c, c), 1), 1.0, 0.0).astype(BF16)
    xns = [_rmsnorm(h_ref[rs, :], g_ref[2:3, :]).astype(BF16) for rs in subs]
    for xn, rs in zip(xns, subs):
        g_low = _dot(xn, wl_ref[...]).astype(BF16)
        z = _dot(g_low, w2_ref[...]) + bg_ref[...]
        la = -_softplus(-z) * (LOG2_E / GLA_GATE_TEMP)
        parts = [_split3(la[r0:r0 + c, :]) for r0 in range(0, PROJ_SUB_ROWS, c)]
        for c0 in range(0, w_ref.shape[1], n_chunk):
            y_ref[rs, c0:c0 + n_chunk] = _dot(xn, w_ref[:, c0:c0 + n_chunk]).astype(y_ref.dtype)
        for i, (hi, mid, lo) in enumerate(parts):
            r0 = rs.start + i * c
            b_ref[r0:r0 + c, :] = _dot(tri, hi) + _dot(tri, mid) + _dot(tri, lo)


def _gla_in(h, g6, w_main, w_low, w_gate2, b_gate):
    t, d = h.shape
    n = w_main.shape[1]
    dk = w_gate2.shape[1]
    tm = PROJ_TOKEN_BLOCK
    return pl.pallas_call(
        functools.partial(_gla_in_body, n_chunk=512),
        grid=(t // tm,),
        in_specs=[
            pl.BlockSpec((tm, d), lambda i: (i, 0)),
            _full(g6.shape),
            _resident(w_main.shape),
            _resident(w_low.shape),
            _resident(w_gate2.shape),
            _full(b_gate.shape),
        ],
        out_specs=[pl.BlockSpec((tm, n), lambda i: (i, 0)), pl.BlockSpec((tm, dk), lambda i: (i, 0))],
        out_shape=[jax.ShapeDtypeStruct((t, n), BF16), jax.ShapeDtypeStruct((t, dk), F32)],
        compiler_params=_params(("parallel",)),
        name="gla_in",
    )(h, g6, w_main, w_low, w_gate2, b_gate)


def _block_first(x, s):
    c, l = x.shape
    xb = x.reshape(c // s, s, l)
    return jnp.broadcast_to(xb[:, 0:1, :], (c // s, s, l)).reshape(c, l)


def _gla_body(q_ref, k_ref, v_ref, go_ref, b_ref, ng_ref, o_ref, st_ref, *, n_chunks, n_heads):
    c = GLA_CHUNK
    dk = q_ref.shape[1] // n_heads
    dv = v_ref.shape[1] // n_heads
    heads = range(n_heads)

    @pl.when(pl.program_id(1) == 0)
    def _():
        st_ref[...] = jnp.zeros_like(st_ref)

    row = lax.broadcasted_iota(jnp.int32, (c, 1), 0)
    ri = lax.broadcasted_iota(jnp.int32, (c, c), 0)
    ci = lax.broadcasted_iota(jnp.int32, (c, c), 1)
    levels = (8, 16, 32)
    odd = [(row // s) % 2 == 1 for s in levels]
    same_group = [ri // (2 * s) == ci // (2 * s) for s in levels]
    hit = [(ci == 8 * (ri // 8) + jj) & (ri % 8 >= jj) for jj in range(8)]

    def chunk(ic, carry):
        rows = pl.ds(pl.multiple_of(ic * c, c), c)
        q, k, v, b = [], [], [], []
        for h in heads:
            q.append(q_ref[rows, h * dk:(h + 1) * dk].astype(F32) * (dk ** -0.5))
            k.append(k_ref[rows, h * dk:(h + 1) * dk].astype(F32))
            v.append(v_ref[rows, h * dv:(h + 1) * dv])
            b.append(b_ref[rows, h * dk:(h + 1) * dk])
        st = [st_ref[h] for h in heads]
        o = [_dot_nt((q[h] * jnp.exp2(b[h])).astype(BF16), st[h].astype(BF16)) for h in heads]
        for h in heads:
            b_last = b[h][c - 1:c, :]
            kd = (k[h] * jnp.exp2(b_last - b[h])).astype(BF16)
            st_ref[h] = st[h] * jnp.exp2(b_last) + _dot_tn(v[h], kd)
        al = []
        for h in heads:
            for i, s in enumerate(levels):
                first = _block_first(b[h], s)
                nxt = jnp.concatenate([first[s:], first[:s]], axis=0)
                qh = jnp.where(odd[i], q[h] * jnp.exp2(b[h] - first), 0.0)
                kh = jnp.where(odd[i], 0.0, k[h] * jnp.exp2(jnp.minimum(nxt - b[h], 0.0)))
                al.append(_dot_nt(qh.astype(BF16), kh.astype(BF16)))
        a = []
        for h in heads:
            ah = jnp.zeros((c, c), F32)
            for i in range(len(levels)):
                ah = ah + jnp.where(same_group[i], al[h * len(levels) + i], 0.0)
            k3 = k[h].reshape(c // 8, 8, dk)
            b3 = b[h].reshape(c // 8, 8, dk)
            for jj in range(8):
                kj = jnp.broadcast_to(k3[:, jj:jj + 1, :], k3.shape).reshape(c, dk)
                bj = jnp.broadcast_to(b3[:, jj:jj + 1, :], b3.shape).reshape(c, dk)
                tt = q[h] * kj * jnp.exp2(jnp.minimum(b[h] - bj, 0.0))
                ah = jnp.where(hit[jj], jnp.sum(tt, axis=-1, keepdims=True), ah)
            a.append(ah.astype(BF16))
        o = [o[h] + _dot(a[h], v[h]) for h in heads]
        for h in heads:
            on = _rmsnorm(o[h], ng_ref[...], GLA_NORM_EPS)
            go = go_ref[rows, h * dv:(h + 1) * dv].astype(F32)
            o_ref[rows, h * dv:(h + 1) * dv] = (on * (go * _sigmoid(go))).astype(o_ref.dtype)
        return carry

    lax.fori_loop(0, n_chunks, chunk, 0, unroll=2)


def _gla_core(y, bcum, norm_g, bsz, seq):
    nh = GLA_HEADS
    dqk = bcum.shape[1]
    dvv = (y.shape[1] - 2 * dqk) // 2
    ts = GLA_SEQ_BLOCK
    ns = seq // ts
    rowblk = lambda b, s: b * ns + s
    return pl.pallas_call(
        functools.partial(_gla_body, n_chunks=ts // GLA_CHUNK, n_heads=nh),
        grid=(bsz, ns),
        in_specs=[
            pl.BlockSpec((ts, dqk), lambda b, s: (rowblk(b, s), 0)),
            pl.BlockSpec((ts, dqk), lambda b, s: (rowblk(b, s), 1)),
            pl.BlockSpec((ts, dvv), lambda b, s: (rowblk(b, s), (2 * dqk) // dvv)),
            pl.BlockSpec((ts, dvv), lambda b, s: (rowblk(b, s), (2 * dqk) // dvv + 1)),
            pl.BlockSpec((ts, dqk), lambda b, s: (rowblk(b, s), 0)),
            _full(norm_g.shape),
        ],
        out_specs=pl.BlockSpec((ts, dvv), lambda b, s: (rowblk(b, s), 0)),
        out_shape=jax.ShapeDtypeStruct((bsz * seq, dvv), BF16),
        scratch_shapes=[pltpu.VMEM((nh, dvv // nh, dqk // nh), F32)],
        compiler_params=_params(("parallel", "arbitrary")),
        name="gla_core",
    )(y, y, y, y, bcum, norm_g)


def _gla_layer(h, g6, w_in, w_gate2, b_gate, norm_g, w_o, bsz, seq):
    d = h.shape[1]
    rank = w_gate2.shape[0]
    n_main = w_in.shape[1] - rank
    w_main = w_in[:, :n_main].astype(BF16)
    w_low = jnp.pad(w_in[:, n_main:], ((0, 0), (0, LANES - rank))).astype(BF16)
    w2 = jnp.pad(w_gate2, ((0, LANES - rank), (0, 0))).astype(BF16)
    y, bcum = _gla_in(h, g6, w_main, w_low, w2, b_gate[None, :])
    o = _gla_core(y, bcum, norm_g[None, :], bsz, seq)
    return o, w_o.astype(BF16), jnp.zeros((1, d), F32)


def _rwkv_pre_body(h_ref, hp_ref, g_ref, mu_ref, wrkv_ref, w1_ref, w2_ref, a1_ref, a2_ref,
                   g1_ref, g2_ref, vec_ref, r_ref, k_ref, v_ref, lw_ref, na_ref, nb_ref, gt_ref,
                   *, blocks_per_seq):
    i = pl.program_id(0)
    tm, d = h_ref.shape
    gain = g_ref[2:3, :]
    x = _rmsnorm(h_ref[...], gain)
    prev = _rmsnorm(hp_ref[...], gain)[7:8, :]
    prev = jnp.where(i % blocks_per_seq == 0, 0.0, prev)
    row = lax.broadcasted_iota(jnp.int32, (tm, 1), 0)
    xs = jnp.where(row == 0, prev, pltpu.roll(x, shift=1, axis=0))
    xx = xs - x
    li = lax.broadcasted_iota(jnp.int32, (LANES, LANES), 0) // RWKV_HEAD
    lj = lax.broadcasted_iota(jnp.int32, (LANES, LANES), 1) // RWKV_HEAD
    gmat = jnp.where(li == lj, 1.0, 0.0).astype(BF16)

    def mix(j):
        return (x + xx * mu_ref[j:j + 1, :]).astype(BF16)

    t_a = _dot(mix(4), a1_ref[...]).astype(BF16)
    t_w = jnp.tanh(_dot(mix(1), w1_ref[...])).astype(BF16)
    t_g = _sigmoid(_dot(mix(5), g1_ref[...])).astype(BF16)
    k = _dot(mix(2), wrkv_ref[1])
    a = _sigmoid(vec_ref[1:2, :] + _dot(t_a, a2_ref[...]))
    kk = k * vec_ref[2:3, :]
    sq = [(kk[:, s:s + LANES] * kk[:, s:s + LANES]).astype(BF16) for s in range(0, d, LANES)]
    v_ref[...] = _dot(mix(3), wrkv_ref[2]).astype(v_ref.dtype)
    k_ref[...] = (k * (1.0 + (a - 1.0) * vec_ref[3:4, :])).astype(k_ref.dtype)
    for i, s in enumerate(range(0, d, LANES)):
        ss = _dot(sq[i], gmat)
        kkn = kk[:, s:s + LANES] / jnp.maximum(jnp.sqrt(ss), 1e-12)
        na_ref[:, s:s + LANES] = (-kkn).astype(na_ref.dtype)
        nb_ref[:, s:s + LANES] = (kkn * a[:, s:s + LANES]).astype(nb_ref.dtype)
    r_ref[...] = _dot(mix(0), wrkv_ref[0]).astype(r_ref.dtype)
    zw = vec_ref[0:1, :] + _dot(t_w, w2_ref[...])
    gt_ref[...] = _dot(t_g, g2_ref[...]).astype(gt_ref.dtype)
    lw_ref[...] = -LOG2_E * jnp.exp(-_softplus(-zw) - 0.5)


def _rwkv_pre(h, g6, mu, wrkv, w1, w2, a1, a2, g1, g2, vec, seq):
    t, d = h.shape
    tm = TOKEN_BLOCK
    tok = pl.BlockSpec((tm, d), lambda i: (i, 0))
    ins = [h, h, g6, mu, wrkv, w1, w2, a1, a2, g1, g2, vec]
    in_specs = [tok, pl.BlockSpec((8, d), lambda i: (jnp.maximum(i * (tm // 8) - 1, 0), 0))]
    in_specs += [_full(x.shape) for x in ins[2:]]
    return pl.pallas_call(
        functools.partial(_rwkv_pre_body, blocks_per_seq=seq // tm),
        grid=(t // tm,),
        in_specs=in_specs,
        out_specs=[tok] * 7,
        out_shape=[jax.ShapeDtypeStruct((t, d), F32 if i == 3 else BF16) for i in range(7)],
        compiler_params=_params(("parallel",)),
        name="rwkv_pre",
    )(*ins)


def _rwkv_body(r_ref, k_ref, v_ref, lw_ref, na_ref, nb_ref, gt_ref, vec_ref, o_ref, s_ref,
               *, n_chunks, n_pairs):
    c = RWKV_CHUNK
    hd = RWKV_HEAD
    n_rows = r_ref.shape[0]
    pairs = range(n_rows * n_pairs)

    def where(p):
        return p // n_pairs, slice((p % n_pairs) * LANES, (p % n_pairs + 1) * LANES)

    @pl.when(pl.program_id(2) == 0)
    def _():
        s_ref[...] = jnp.zeros_like(s_ref)

    lane = lax.broadcasted_iota(jnp.int32, (c, LANES), 1)
    head0 = lane < hd
    ri = lax.broadcasted_iota(jnp.int32, (2 * c, 2 * c), 0)
    ci = lax.broadcasted_iota(jnp.int32, (2 * c, 2 * c), 1)
    strict = ri > ci
    incl = ri >= ci
    tri = jnp.where(lax.broadcasted_iota(jnp.int32, (c, c), 0)
                    >= lax.broadcasted_iota(jnp.int32, (c, c), 1), 1.0, 0.0).astype(BF16)

    def stack(x):
        return jnp.concatenate([jnp.where(head0, x, 0.0), jnp.where(head0, 0.0, x)], axis=0)

    def hsum(x):
        s0 = jnp.sum(jnp.where(head0, x, 0.0), axis=-1, keepdims=True)
        s1 = jnp.sum(jnp.where(head0, 0.0, x), axis=-1, keepdims=True)
        return jnp.where(head0, s0, s1)

    def chunk(ic, carry):
        r0 = pl.multiple_of(ic * c, c)
        rows = pl.ds(r0, c)
        cl_all = []
        for bb in range(n_rows):
            hi, mid, lo = _split3(lw_ref[bb, rows, :])
            cl_all.append(_dot(tri, hi) + _dot(tri, mid) + _dot(tri, lo))
        lhs, lhs_flat, rhs, rhs_end, vsb, sb, decay = [], [], [], [], [], [], []
        for p in pairs:
            bb, sl = where(p)
            r, k, v = (x[bb, rows, sl].astype(F32) for x in (r_ref, k_ref, v_ref))
            lw = lw_ref[bb, rows, sl]
            na, nb = na_ref[bb, rows, sl].astype(F32), nb_ref[bb, rows, sl].astype(F32)
            cl = cl_all[bb][:, sl]
            cl_last = cl[c - 1:c, :]
            p_inv = jnp.exp2(-cl)
            p_end = jnp.exp2(cl_last - cl)
            at, rt = na * jnp.exp2(cl - lw), r * jnp.exp2(cl)
            lhs.append(jnp.concatenate([stack(at), stack(rt)], axis=0).astype(BF16))
            lhs_flat.append(jnp.concatenate([at, rt], axis=0).astype(BF16))
            rhs.append(jnp.concatenate([stack(nb * p_inv), stack(k * p_inv)], axis=0).astype(BF16))
            rhs_end.append(jnp.concatenate([stack(nb * p_end), stack(k * p_end)], axis=0).astype(BF16))
            vsb.append(stack(v).astype(BF16))
            sb.append(s_ref[p].astype(BF16))
            decay.append(jnp.exp2(cl_last))
        gram = [_dot_nt(lhs[p], rhs[p]) for p in pairs]
        ws = [_dot_nt(lhs_flat[p], sb[p]) for p in pairs]
        a_ab = [jnp.where(strict, gram[p][:2 * c, :2 * c], 0.0) for p in pairs]
        a_ak = [jnp.where(strict, gram[p][:2 * c, 2 * c:], 0.0).astype(BF16) for p in pairs]
        a_r = [jnp.concatenate([jnp.where(incl, gram[p][2 * c:, :2 * c], 0.0),
                                jnp.where(incl, gram[p][2 * c:, 2 * c:], 0.0)], axis=1).astype(BF16)
               for p in pairs]
        u = [stack(ws[p][:c]) + _dot(a_ak[p], vsb[p]) for p in pairs]
        apow = a_ab
        steps = (c - 1).bit_length()
        for it in range(steps):
            ab = [apow[p].astype(BF16) for p in pairs]
            if it + 1 < steps:
                both = [_dot(ab[p], jnp.concatenate([u[p].astype(BF16), ab[p]], axis=1))
                        for p in pairs]
                u = [u[p] + both[p][:, :LANES] for p in pairs]
                apow = [both[p][:, LANES:] for p in pairs]
            else:
                u = [u[p] + _dot(ab[p], u[p].astype(BF16)) for p in pairs]
        uv = [jnp.concatenate([u[p].astype(BF16), vsb[p]], axis=0) for p in pairs]
        ys = [_dot(a_r[p], uv[p]) for p in pairs]
        for p in pairs:
            s_ref[p] = s_ref[p] * decay[p] + _dot_tn(uv[p], rhs_end[p])
        for p in pairs:
            bb, sl = where(p)
            y = ys[p][:c, :] + ys[p][c:, :] + ws[p][c:, :]
            mean = hsum(y) * (1.0 / hd)
            yc = y - mean
            var = hsum(yc * yc) * (1.0 / hd)
            yn = yc * lax.rsqrt(var + RWKV_LNX_EPS) * vec_ref[1:2, sl] + vec_ref[2:3, sl]
            r, k, v = (x[bb, rows, sl].astype(F32) for x in (r_ref, k_ref, v_ref))
            yn = yn + hsum(r * k * vec_ref[0:1, sl]) * v
            o_ref[bb, rows, sl] = (yn * gt_ref[bb, rows, sl].astype(F32)).astype(o_ref.dtype)
        return carry

    lax.fori_loop(0, n_chunks, chunk, 0, unroll=2)


def _rwkv_core(r, k, v, lw, na, nb, gt, vec, bsz, seq):
    t, d = r.shape
    ts = RWKV_SEQ_BLOCK
    nr = RWKV_BATCH_ROWS
    width = RWKV_PAIRS * LANES
    tok = pl.BlockSpec((nr, ts, width), lambda b, p, s: (b, s, p))
    seqs = [x.reshape(bsz, seq, d) for x in (r, k, v, lw, na, nb, gt)]
    out = pl.pallas_call(
        functools.partial(_rwkv_body, n_chunks=ts // RWKV_CHUNK, n_pairs=RWKV_PAIRS),
        grid=(bsz // nr, d // width, seq // ts),
        in_specs=[tok] * 7 + [pl.BlockSpec((vec.shape[0], width), lambda b, p, s: (0, p))],
        out_specs=tok,
        out_shape=jax.ShapeDtypeStruct((bsz, seq, d), BF16),
        scratch_shapes=[pltpu.VMEM((nr * RWKV_PAIRS, LANES, LANES), F32)],
        compiler_params=_params(("parallel", "parallel", "arbitrary")),
        name="rwkv_core",
    )(*seqs, vec)
    return out.reshape(t, d)


def _pad_to(x, axis, size):
    pads = [(0, 0)] * x.ndim
    pads[axis] = (0, size - x.shape[axis])
    return jnp.pad(x, pads)


def _rwkv_layer(h, g6, mu, w_rkv, w0, w1, w2, a0, a1, a2, g1, g2, k_k, k_a, r_k, lnx_g, lnx_b,
                w_o, bsz, seq):
    d = h.shape[1]
    up = lambda n: -(-n // LANES) * LANES
    w1p = _pad_to(w1, 1, up(w1.shape[1])).astype(BF16)
    w2p = _pad_to(w2, 0, up(w2.shape[0])).astype(BF16)
    a1p = _pad_to(a1, 1, up(a1.shape[1])).astype(BF16)
    a2p = _pad_to(a2, 0, up(a2.shape[0])).astype(BF16)
    g1p = _pad_to(g1, 1, up(g1.shape[1])).astype(BF16)
    g2p = _pad_to(g2, 0, up(g2.shape[0])).astype(BF16)
    vec_pre = _pad_to(jnp.stack([w0, a0, k_k, k_a]), 0, 8)
    outs = _rwkv_pre(h, g6, _pad_to(mu, 0, 8), w_rkv.astype(BF16), w1p, w2p, a1p, a2p, g1p, g2p,
                     vec_pre, seq)
    vec_core = _pad_to(jnp.stack([r_k.reshape(d), lnx_g, lnx_b]), 0, 8)
    o = _rwkv_core(*outs, vec_core, bsz, seq)
    return o, w_o.astype(BF16), jnp.zeros((1, d), F32)


def kernel(x, norm_g, ffn_w_in, ffn_w_out, swa_w_qkv, swa_b_qkv, swa_sinks, swa_w_o, swa_b_o, gla_w_in, gla_w_gate2, gla_b_gate, gla_norm_g, gla_w_o, rwkv_mu, rwkv_w_rkv, rwkv_w0, rwkv_w1, rwkv_w2, rwkv_a0, rwkv_a1, rwkv_a2, rwkv_g1, rwkv_g2, rwkv_k_k, rwkv_k_a, rwkv_r_k, rwkv_lnx_g, rwkv_lnx_b, rwkv_w_o):
    bsz, seq, d = x.shape
    depth = norm_g.shape[0]
    h = x.reshape(bsz * seq, d)
    win, wout = ffn_w_in.astype(BF16), ffn_w_out.astype(BF16)
    for layer in range(depth):
        g6 = norm_g[layer]
        h = _ffn(h, g6, win, wout, layer, 0)
        kind, j = layer % 3, layer // 3
        if kind == 0:
            mix = _swa_layer(h, g6, swa_w_qkv[j], swa_b_qkv[j], swa_sinks[j], swa_w_o[j],
                             swa_b_o[j], bsz, seq)
        elif kind == 1:
            mix = _gla_layer(h, g6, gla_w_in[j], gla_w_gate2[j], gla_b_gate[j], gla_norm_g[j],
                             gla_w_o[j], bsz, seq)
        else:
            mix = _rwkv_layer(h, g6, rwkv_mu[j], rwkv_w_rkv[j], rwkv_w0[j], rwkv_w1[j], rwkv_w2[j],
                              rwkv_a0[j], rwkv_a1[j], rwkv_a2[j], rwkv_g1[j], rwkv_g2[j],
                              rwkv_k_k[j], rwkv_k_a[j], rwkv_r_k[j], rwkv_lnx_g[j], rwkv_lnx_b[j],
                              rwkv_w_o[j], bsz, seq)
        h = _ffn(h, g6, win, wout, layer, 1, mixer=mix)
    return h.reshape(bsz, seq, d)
```

```python
import functools

import jax
import jax.numpy as jnp
from jax import lax
from jax.experimental import pallas as pl
from jax.experimental.pallas import tpu as pltpu

F32 = jnp.float32
BF16 = jnp.bfloat16

NORM_EPS = 1e-6
GLA_NORM_EPS = 1e-5
GLA_GATE_TEMP = 16.0
RWKV_LNX_EPS = 64e-5
LOG2_E = 1.4426950408889634

D_FF_CHUNK = 256
TOKEN_BLOCK = 512
PROJ_TOKEN_BLOCK = 1024
PROJ_SUB_ROWS = 512
FFN_TOKEN_BLOCK = 1024
FFN_SUB_ROWS = 512
SWA_BLOCK = 128
SWA_Q_BLOCKS = 8
SWA_HEAD_DIM = 64
SWA_HEADS = 16
SWA_KV_HEADS = 4
GLA_HEADS = 4
GLA_CHUNK = 64
GLA_SEQ_BLOCK = 512
RWKV_HEAD = 64
RWKV_CHUNK = 64
RWKV_SEQ_BLOCK = 512
RWKV_PAIRS = 8
RWKV_BATCH_ROWS = 2
LANES = 128
VMEM_LIMIT = 56 * 1024 * 1024


def _params(sem):
    return pltpu.CompilerParams(dimension_semantics=sem, vmem_limit_bytes=VMEM_LIMIT)


def _rmsnorm(x, g, eps=NORM_EPS):
    return x * lax.rsqrt(jnp.mean(x * x, axis=-1, keepdims=True) + eps) * g


def _sigmoid(x):
    return 1.0 / (1.0 + jnp.exp(-x))


def _softplus(x):
    return jnp.maximum(x, 0.0) + jnp.log(1.0 + jnp.exp(-jnp.abs(x)))


def _dot(a, b):
    return jnp.dot(a, b, preferred_element_type=F32)


def _dot_nt(a, b):
    return lax.dot_general(a, b, (((1,), (1,)), ((), ())), preferred_element_type=F32)


def _dot_tn(a, b):
    return lax.dot_general(a, b, (((0,), (0,)), ((), ())), preferred_element_type=F32)


def _full(shape):
    n = len(shape)
    return pl.BlockSpec(shape, lambda *_: (0,) * n)


def _ffn_body(*refs, g_row, mixer_proj):
    if mixer_proj:
        h_ref, a_ref, wo_ref, bo_ref, g_ref, win_ref, wout_ref, o_ref, xn_ref, acc_ref = refs
    else:
        h_ref, g_ref, win_ref, wout_ref, o_ref, xn_ref, acc_ref = refs
    d_ff = wout_ref.shape[0]
    tf = D_FF_CHUNK
    tm = h_ref.shape[0]
    subs = [slice(r0, r0 + FFN_SUB_ROWS) for r0 in range(0, tm, FFN_SUB_ROWS)]
    for rs in subs:
        hs = h_ref[rs, :]
        if mixer_proj:
            m = _dot(a_ref[rs, :], wo_ref[...]) + bo_ref[...]
            hs = hs + _rmsnorm(m, g_ref[g_row - 1:g_row, :])
            o_ref[rs, :] = hs
        xn_ref[rs, :] = _rmsnorm(hs, g_ref[g_row:g_row + 1, :]).astype(BF16)
    res_ref = o_ref if mixer_proj else h_ref
    for rs in subs:
        for c0 in range(0, d_ff, tf):
            xn = xn_ref[rs, :]
            gate = _dot(xn, win_ref[:, c0:c0 + tf].astype(BF16))
            up = _dot(xn, win_ref[:, d_ff + c0:d_ff + c0 + tf].astype(BF16))
            act = (gate * _sigmoid(gate) * up).astype(BF16)
            part = _dot(act, wout_ref[c0:c0 + tf, :].astype(BF16))
            if c0 == 0:
                acc_ref[rs, :] = part
            else:
                acc_ref[rs, :] += part
    for rs in subs:
        o_ref[rs, :] = res_ref[rs, :] + 0.5 * _rmsnorm(acc_ref[rs, :], g_ref[g_row + 1:g_row + 2, :])


def _resident(shape):
    n = len(shape)
    return pl.BlockSpec(shape, lambda *_: (0,) * n, pipeline_mode=pl.Buffered(1))


def _stacked_weight(w, *lead):
    idx = tuple(lead) + (0, 0)
    return pl.BlockSpec((None,) * len(lead) + w.shape[-2:], lambda *_: idx,
                        pipeline_mode=pl.Buffered(1))


def _ffn(h, g6, win, wout, layer, which, mixer=None):
    t, d = h.shape
    g_row = 4 * which
    tm = FFN_TOKEN_BLOCK
    tok = pl.BlockSpec((tm, d), lambda i: (i, 0))
    ins, in_specs = [h], [tok]
    if mixer is not None:
        a, w_o, b_o = mixer
        ins += [a, w_o, b_o]
        in_specs += [pl.BlockSpec((tm, a.shape[1]), lambda i: (i, 0)), _resident(w_o.shape),
                     _full(b_o.shape)]
    ins += [g6, win, wout]
    win_lead = (layer, which) if win.ndim == 4 else (layer,)
    in_specs += [_full(g6.shape), _stacked_weight(win, *win_lead),
                 _stacked_weight(wout, layer, which)]
    return pl.pallas_call(
        functools.partial(_ffn_body, g_row=g_row, mixer_proj=mixer is not None),
        grid=(t // tm,),
        in_specs=in_specs,
        out_specs=tok,
        out_shape=jax.ShapeDtypeStruct((t, d), F32),
        scratch_shapes=[pltpu.VMEM((tm, d), BF16), pltpu.VMEM((tm, d), F32)],
        compiler_params=_params(("parallel",)),
        name="ffn_mix" if mixer is not None else "ffn",
    )(*ins)


def _row_subs(tm):
    return [slice(r0, r0 + PROJ_SUB_ROWS) for r0 in range(0, tm, PROJ_SUB_ROWS)]


def _norm_proj_body(h_ref, g_ref, w_ref, b_ref, *o_refs, g_row, n_chunk):
    subs = _row_subs(h_ref.shape[0])
    xns = [_rmsnorm(h_ref[rs, :], g_ref[g_row:g_row + 1, :]).astype(BF16) for rs in subs]
    for xn, rs in zip(xns, subs):
        col = 0
        for o_ref in o_refs:
            n = o_ref.shape[1]
            for c in range(0, n, n_chunk):
                cs = slice(col + c, col + min(c + n_chunk, n))
                y = _dot(xn, w_ref[:, cs]) + b_ref[:, cs]
                o_ref[rs, c:c + cs.stop - cs.start] = y.astype(o_ref.dtype)
            col += n


def _norm_proj(h, g6, g_row, w, b, widths):
    t, d = h.shape
    tm = PROJ_TOKEN_BLOCK
    return pl.pallas_call(
        functools.partial(_norm_proj_body, g_row=g_row, n_chunk=512),
        grid=(t // tm,),
        in_specs=[
            pl.BlockSpec((tm, d), lambda i: (i, 0)),
            _full(g6.shape),
            _resident(w.shape),
            _full(b.shape),
        ],
        out_specs=[pl.BlockSpec((tm, n), lambda i: (i, 0)) for n in widths],
        out_shape=[jax.ShapeDtypeStruct((t, n), BF16) for n in widths],
        compiler_params=_params(("parallel",)),
        name="norm_proj",
    )(h, g6, w, b)


def _swa_body(sink_ref, q_ref, kp_ref, kc_ref, vp_ref, vc_ref, o_ref):
    first_step = pl.program_id(1) == 0
    blk = SWA_BLOCK
    nkv = SWA_KV_HEADS
    nq = q_ref.shape[0] // blk
    kall = jnp.concatenate([kp_ref[...], kc_ref[...]], axis=0)
    vall = jnp.concatenate([vp_ref[...], vc_ref[...]], axis=0)
    own = (lax.broadcasted_iota(jnp.int32, (blk, blk), 1)
           <= lax.broadcasted_iota(jnp.int32, (blk, blk), 0))
    no_prev = jnp.where(first_step, -jnp.inf, 0.0)
    width = nkv * SWA_HEAD_DIM
    lane_head = lax.broadcasted_iota(jnp.int32, kall.shape, 1) // SWA_HEAD_DIM
    scale = SWA_HEAD_DIM ** -0.5
    groups = SWA_HEADS // nkv
    kmask = [kall * jnp.where(lane_head == hh, scale, 0.0).astype(BF16) for hh in range(nkv)]
    vmask = [vall * jnp.where(lane_head == hh, 1.0, 0.0).astype(BF16) for hh in range(nkv)]
    work = [(j, s) for j in range(nq) for s in range(groups)]

    def keys_of(masked, j):
        return jnp.concatenate([m[j * blk:(j + 2) * blk] for m in masked], axis=0)

    def softmax(j, s, sc4):
        ps = []
        for hh in range(nkv):
            c0 = hh * 2 * blk
            prev = sc4[:, c0:c0 + blk]
            if j == 0:
                prev = prev + no_prev
            sc = jnp.where(own, sc4[:, c0 + blk:c0 + 2 * blk], prev)
            sink = sink_ref[hh * groups + s]
            m = jnp.maximum(jnp.max(sc, axis=-1, keepdims=True), sink)
            p = jnp.exp(sc - m)
            inv = 1.0 / (jnp.sum(p, axis=-1, keepdims=True) + jnp.exp(sink - m))
            p = p * inv
            ps.append(jnp.where(own, 0.0, p).astype(BF16))
            ps.append(jnp.where(own, p, 0.0).astype(BF16))
        return jnp.concatenate(ps, axis=1)

    kstack = [keys_of(kmask, j) for j in range(nq)]
    vstack = [keys_of(vmask, j) for j in range(nq)]
    scores = [_dot_nt(q_ref[j * blk:(j + 1) * blk, s * width:(s + 1) * width], kstack[j])
              for j, s in work]
    probs = [softmax(j, s, scores[i]) for i, (j, s) in enumerate(work)]
    for i, (j, s) in enumerate(work):
        o_ref[j * blk:(j + 1) * blk, s * width:(s + 1) * width] = (
            _dot(probs[i], vstack[j]).astype(o_ref.dtype))


def _swa_core(q, k, v, sinks, bsz, seq):
    blk = SWA_BLOCK
    nq = SWA_Q_BLOCKS
    ns = seq // (nq * blk)
    dq, dkv = q.shape[1], k.shape[1]
    cur = lambda b, n: (b * ns + n, 0)
    prev = lambda b, n: ((b * ns + n) * nq - jnp.minimum(n, 1), 0)
    return pl.pallas_call(
        _swa_body,
        grid=(bsz, ns),
        in_specs=[
            pl.BlockSpec(memory_space=pltpu.SMEM),
            pl.BlockSpec((nq * blk, dq), cur),
            pl.BlockSpec((blk, dkv), prev),
            pl.BlockSpec((nq * blk, dkv), cur),
            pl.BlockSpec((blk, dkv), prev),
            pl.BlockSpec((nq * blk, dkv), cur),
        ],
        out_specs=pl.BlockSpec((nq * blk, dq), cur),
        out_shape=jax.ShapeDtypeStruct((bsz * seq, dq), BF16),
        compiler_params=_params(("parallel", "parallel")),
        name="swa_core",
    )(sinks, q, k, k, v, v)


def _swa_layer(h, g6, w_qkv, b_qkv, sinks, w_o, b_o, bsz, seq):
    d = h.shape[1]
    hd, nh, nkv = SWA_HEAD_DIM, SWA_HEADS, SWA_KV_HEADS
    grp = nh // nkv
    dq = nh * hd
    wq = w_qkv[:, :dq].reshape(d, nkv, grp, hd).transpose(0, 2, 1, 3).reshape(d, dq)
    bq = b_qkv[:dq].reshape(nkv, grp, hd).transpose(1, 0, 2).reshape(dq)
    w = jnp.concatenate([wq, w_qkv[:, dq:]], axis=1).astype(BF16)
    b = jnp.concatenate([bq, b_qkv[dq:]])[None, :]
    wo = w_o.reshape(nkv, grp, hd, d).transpose(1, 0, 2, 3).reshape(dq, d).astype(BF16)
    dkv = nkv * hd
    q, k, v = _norm_proj(h, g6, 2, w, b, (dq, dkv, dkv))
    att = _swa_core(q, k, v, sinks, bsz, seq)
    return att, wo, b_o[None, :]


def _split3(x):
    hi = x.astype(BF16)
    r1 = x - hi.astype(F32)
    mid = r1.astype(BF16)
    lo = (r1 - mid.astype(F32)).astype(BF16)
    return hi, mid, lo


def _gla_in_body(h_ref, g_ref, w_ref, wl_ref, w2_ref, bg_ref, y_ref, b_ref, *, n_chunk):
    c = GLA_CHUNK
    subs = _row_subs(h_ref.shape[0])
    tri = jnp.where(lax.broadcasted_iota(jnp.int32, (c, c), 0)
                    >= lax.broadcasted_iota(jnp.int32, (c, c), 1), 1.0, 0.0).astype(BF16)
    xns = [_rmsnorm(h_ref[rs, :], g_ref[2:3, :]).astype(BF16) for rs in subs]
    for xn, rs in zip(xns, subs):
        g_low = _dot(xn, wl_ref[...]).astype(BF16)
        z = _dot(g_low, w2_ref[...]) + bg_ref[...]
        la = -_softplus(-z) * (LOG2_E / GLA_GATE_TEMP)
        parts = [_split3(la[r0:r0 + c, :]) for r0 in range(0, PROJ_SUB_ROWS, c)]
        for c0 in range(0, w_ref.shape[1], n_chunk):
            y_ref[rs, c0:c0 + n_chunk] = _dot(xn, w_ref[:, c0:c0 + n_chunk]).astype(y_ref.dtype)
        for i, (hi, mid, lo) in enumerate(parts):
            r0 = rs.start + i * c
            b_ref[r0:r0 + c, :] = _dot(tri, hi) + _dot(tri, mid) + _dot(tri, lo)


def _gla_in(h, g6, w_main, w_low, w_gate2, b_gate):
    t, d = h.shape
    n = w_main.shape[1]
    dk = w_gate2.shape[1]
    tm = PROJ_TOKEN_BLOCK
    return pl.pallas_call(
        functools.partial(_gla_in_body, n_chunk=512),
        grid=(t // tm,),
        in_specs=[
            pl.BlockSpec((tm, d), lambda i: (i, 0)),
            _full(g6.shape),
            _resident(w_main.shape),
            _resident(w_low.shape),
            _resident(w_gate2.shape),
            _full(b_gate.shape),
        ],
        out_specs=[pl.BlockSpec((tm, n), lambda i: (i, 0)), pl.BlockSpec((tm, dk), lambda i: (i, 0))],
        out_shape=[jax.ShapeDtypeStruct((t, n), BF16), jax.ShapeDtypeStruct((t, dk), F32)],
        compiler_params=_params(("parallel",)),
        name="gla_in",
    )(h, g6, w_main, w_low, w_gate2, b_gate)


def _block_first(x, s):
    c, l = x.shape
    xb = x.reshape(c // s, s, l)
    return jnp.broadcast_to(xb[:, 0:1, :], (c // s, s, l)).reshape(c, l)


def _gla_body(q_ref, k_ref, v_ref, go_ref, b_ref, ng_ref, o_ref, st_ref, *, n_chunks, n_heads):
    c = GLA_CHUNK
    dk = q_ref.shape[1] // n_heads
    dv = v_ref.shape[1] // n_heads
    heads = range(n_heads)

    @pl.when(pl.program_id(1) == 0)
    def _():
        st_ref[...] = jnp.zeros_like(st_ref)

    row = lax.broadcasted_iota(jnp.int32, (c, 1), 0)
    ri = lax.broadcasted_iota(jnp.int32, (c, c), 0)
    ci = lax.broadcasted_iota(jnp.int32, (c, c), 1)
    levels = (8, 16, 32)
    odd = [(row // s) % 2 == 1 for s in levels]
    same_group = [ri // (2 * s) == ci // (2 * s) for s in levels]
    hit = [(ci == 8 * (ri // 8) + jj) & (ri % 8 >= jj) for jj in range(8)]

    def chunk(ic, carry):
        rows = pl.ds(pl.multiple_of(ic * c, c), c)
        q, k, v, b = [], [], [], []
        for h in heads:
            q.append(q_ref[rows, h * dk:(h + 1) * dk].astype(F32) * (dk ** -0.5))
            k.append(k_ref[rows, h * dk:(h + 1) * dk].astype(F32))
            v.append(v_ref[rows, h * dv:(h + 1) * dv])
            b.append(b_ref[rows, h * dk:(h + 1) * dk])
        st = [st_ref[h] for h in heads]
        o = [_dot_nt((q[h] * jnp.exp2(b[h])).astype(BF16), st[h].astype(BF16)) for h in heads]
        for h in heads:
            b_last = b[h][c - 1:c, :]
            kd = (k[h] * jnp.exp2(b_last - b[h])).astype(BF16)
            st_ref[h] = st[h] * jnp.exp2(b_last) + _dot_tn(v[h], kd)
        al = []
        for h in heads:
            for i, s in enumerate(levels):
                first = _block_first(b[h], s)
                nxt = jnp.concatenate([first[s:], first[:s]], axis=0)
                qh = jnp.where(odd[i], q[h] * jnp.exp2(b[h] - first), 0.0)
                kh = jnp.where(odd[i], 0.0, k[h] * jnp.exp2(jnp.minimum(nxt - b[h], 0.0)))
                al.append(_dot_nt(qh.astype(BF16), kh.astype(BF16)))
        a = []
        for h in heads:
            ah = jnp.zeros((c, c), F32)
            for i in range(len(levels)):
                ah = ah + jnp.where(same_group[i], al[h * len(levels) + i], 0.0)
            k3 = k[h].reshape(c // 8, 8, dk)
            b3 = b[h].reshape(c // 8, 8, dk)
            for jj in range(8):
                kj = jnp.broadcast_to(k3[:, jj:jj + 1, :], k3.shape).reshape(c, dk)
                bj = jnp.broadcast_to(b3[:, jj:jj + 1, :], b3.shape).reshape(c, dk)
                tt = q[h] * kj * jnp.exp2(jnp.minimum(b[h] - bj, 0.0))
                ah = jnp.where(hit[jj], jnp.sum(tt, axis=-1, keepdims=True), ah)
            a.append(ah.astype(BF16))
        o = [o[h] + _dot(a[h], v[h]) for h in heads]
        for h in heads:
            on = _rmsnorm(o[h], ng_ref[...], GLA_NORM_EPS)
            go = go_ref[rows, h * dv:(h + 1) * dv].astype(F32)
            o_ref[rows, h * dv:(h + 1) * dv] = (on * (go * _sigmoid(go))).astype(o_ref.dtype)
        return carry

    lax.fori_loop(0, n_chunks, chunk, 0, unroll=2)


def _gla_core(y, bcum, norm_g, bsz, seq):
    nh = GLA_HEADS
    dqk = bcum.shape[1]
    dvv = (y.shape[1] - 2 * dqk) // 2
    ts = GLA_SEQ_BLOCK
    ns = seq // ts
    rowblk = lambda b, s: b * ns + s
    return pl.pallas_call(
        functools.partial(_gla_body, n_chunks=ts // GLA_CHUNK, n_heads=nh),
        grid=(bsz, ns),
        in_specs=[
            pl.BlockSpec((ts, dqk), lambda b, s: (rowblk(b, s), 0)),
            pl.BlockSpec((ts, dqk), lambda b, s: (rowblk(b, s), 1)),
            pl.BlockSpec((ts, dvv), lambda b, s: (rowblk(b, s), (2 * dqk) // dvv)),
            pl.BlockSpec((ts, dvv), lambda b, s: (rowblk(b, s), (2 * dqk) // dvv + 1)),
            pl.BlockSpec((ts, dqk), lambda b, s: (rowblk(b, s), 0)),
            _full(norm_g.shape),
        ],
        out_specs=pl.BlockSpec((ts, dvv), lambda b, s: (rowblk(b, s), 0)),
        out_shape=jax.ShapeDtypeStruct((bsz * seq, dvv), BF16),
        scratch_shapes=[pltpu.VMEM((nh, dvv // nh, dqk // nh), F32)],
        compiler_params=_params(("parallel", "arbitrary")),
        name="gla_core",
    )(y, y, y, y, bcum, norm_g)


def _gla_layer(h, g6, w_in, w_gate2, b_gate, norm_g, w_o, bsz, seq):
    d = h.shape[1]
    rank = w_gate2.shape[0]
    n_main = w_in.shape[1] - rank
    w_main = w_in[:, :n_main].astype(BF16)
    w_low = jnp.pad(w_in[:, n_main:], ((0, 0), (0, LANES - rank))).astype(BF16)
    w2 = jnp.pad(w_gate2, ((0, LANES - rank), (0, 0))).astype(BF16)
    y, bcum = _gla_in(h, g6, w_main, w_low, w2, b_gate[None, :])
    o = _gla_core(y, bcum, norm_g[None, :], bsz, seq)
    return o, w_o.astype(BF16), jnp.zeros((1, d), F32)


def _rwkv_pre_body(h_ref, hp_ref, g_ref, mu_ref, wrkv_ref, w1_ref, w2_ref, a1_ref, a2_ref,
                   g1_ref, g2_ref, vec_ref, r_ref, k_ref, v_ref, lw_ref, na_ref, nb_ref, gt_ref,
                   *, blocks_per_seq):
    i = pl.program_id(0)
    tm, d = h_ref.shape
    gain = g_ref[2:3, :]
    x = _rmsnorm(h_ref[...], gain)
    prev = _rmsnorm(hp_ref[...], gain)[7:8, :]
    prev = jnp.where(i % blocks_per_seq == 0, 0.0, prev)
    row = lax.broadcasted_iota(jnp.int32, (tm, 1), 0)
    xs = jnp.where(row == 0, prev, pltpu.roll(x, shift=1, axis=0))
    xx = xs - x
    li = lax.broadcasted_iota(jnp.int32, (LANES, LANES), 0) // RWKV_HEAD
    lj = lax.broadcasted_iota(jnp.int32, (LANES, LANES), 1) // RWKV_HEAD
    gmat = jnp.where(li == lj, 1.0, 0.0).astype(BF16)

    def mix(j):
        return (x + xx * mu_ref[j:j + 1, :]).astype(BF16)

    t_a = _dot(mix(4), a1_ref[...]).astype(BF16)
    t_w = jnp.tanh(_dot(mix(1), w1_ref[...])).astype(BF16)
    t_g = _sigmoid(_dot(mix(5), g1_ref[...])).astype(BF16)
    k = _dot(mix(2), wrkv_ref[1])
    a = _sigmoid(vec_ref[1:2, :] + _dot(t_a, a2_ref[...]))
    kk = k * vec_ref[2:3, :]
    sq = [(kk[:, s:s + LANES] * kk[:, s:s + LANES]).astype(BF16) for s in range(0, d, LANES)]
    v_ref[...] = _dot(mix(3), wrkv_ref[2]).astype(v_ref.dtype)
    k_ref[...] = (k * (1.0 + (a - 1.0) * vec_ref[3:4, :])).astype(k_ref.dtype)
    for i, s in enumerate(range(0, d, LANES)):
        ss = _dot(sq[i], gmat)
        kkn = kk[:, s:s + LANES] / jnp.maximum(jnp.sqrt(ss), 1e-12)
        na_ref[:, s:s + LANES] = (-kkn).astype(na_ref.dtype)
        nb_ref[:, s:s + LANES] = (kkn * a[:, s:s + LANES]).astype(nb_ref.dtype)
    r_ref[...] = _dot(mix(0), wrkv_ref[0]).astype(r_ref.dtype)
    zw = vec_ref[0:1, :] + _dot(t_w, w2_ref[...])
    gt_ref[...] = _dot(t_g, g2_ref[...]).astype(gt_ref.dtype)
    lw_ref[...] = -LOG2_E * jnp.exp(-_softplus(-zw) - 0.5)


def _rwkv_pre(h, g6, mu, wrkv, w1, w2, a1, a2, g1, g2, vec, seq):
    t, d = h.shape
    tm = TOKEN_BLOCK
    tok = pl.BlockSpec((tm, d), lambda i: (i, 0))
    ins = [h, h, g6, mu, wrkv, w1, w2, a1, a2, g1, g2, vec]
    in_specs = [tok, pl.BlockSpec((8, d), lambda i: (jnp.maximum(i * (tm // 8) - 1, 0), 0))]
    in_specs += [_full(x.shape) for x in ins[2:]]
    return pl.pallas_call(
        functools.partial(_rwkv_pre_body, blocks_per_seq=seq // tm),
        grid=(t // tm,),
        in_specs=in_specs,
        out_specs=[tok] * 7,
        out_shape=[jax.ShapeDtypeStruct((t, d), F32 if i == 3 else BF16) for i in range(7)],
        compiler_params=_params(("parallel",)),
        name="rwkv_pre",
    )(*ins)


def _rwkv_body(r_ref, k_ref, v_ref, lw_ref, na_ref, nb_ref, gt_ref, vec_ref, o_ref, s_ref,
               *, n_chunks, n_pairs):
    c = RWKV_CHUNK
    hd = RWKV_HEAD
    n_rows = r_ref.shape[0]
    pairs = range(n_rows * n_pairs)

    def where(p):
        return p // n_pairs, slice((p % n_pairs) * LANES, (p % n_pairs + 1) * LANES)

    @pl.when(pl.program_id(2) == 0)
    def _():
        s_ref[...] = jnp.zeros_like(s_ref)

    lane = lax.broadcasted_iota(jnp.int32, (c, LANES), 1)
    head0 = lane < hd
    ri = lax.broadcasted_iota(jnp.int32, (2 * c, 2 * c), 0)
    ci = lax.broadcasted_iota(jnp.int32, (2 * c, 2 * c), 1)
    strict = ri > ci
    incl = ri >= ci
    tri = jnp.where(lax.broadcasted_iota(jnp.int32, (c, c), 0)
                    >= lax.broadcasted_iota(jnp.int32, (c, c), 1), 1.0, 0.0).astype(BF16)

    def stack(x):
        return jnp.concatenate([jnp.where(head0, x, 0.0), jnp.where(head0, 0.0, x)], axis=0)

    def hsum(x):
        s0 = jnp.sum(jnp.where(head0, x, 0.0), axis=-1, keepdims=True)
        s1 = jnp.sum(jnp.where(head0, 0.0, x), axis=-1, keepdims=True)
        return jnp.where(head0, s0, s1)

    def chunk(ic, carry):
        r0 = pl.multiple_of(ic * c, c)
        rows = pl.ds(r0, c)
        cl_all = []
        for bb in range(n_rows):
            hi, mid, lo = _split3(lw_ref[bb, rows, :])
            cl_all.append(_dot(tri, hi) + _dot(tri, mid) + _dot(tri, lo))
        lhs, lhs_flat, rhs, rhs_end, vsb, sb, decay = [], [], [], [], [], [], []
        for p in pairs:
            bb, sl = where(p)
            r, k, v = (x[bb, rows, sl].astype(F32) for x in (r_ref, k_ref, v_ref))
            lw = lw_ref[bb, rows, sl]
            na, nb = na_ref[bb, rows, sl].astype(F32), nb_ref[bb, rows, sl].astype(F32)
            cl = cl_all[bb][:, sl]
            cl_last = cl[c - 1:c, :]
            p_inv = jnp.exp2(-cl)
            p_end = jnp.exp2(cl_last - cl)
            at, rt = na * jnp.exp2(cl - lw), r * jnp.exp2(cl)
            lhs.append(jnp.concatenate([stack(at), stack(rt)], axis=0).astype(BF16))
            lhs_flat.append(jnp.concatenate([at, rt], axis=0).astype(BF16))
            rhs.append(jnp.concatenate([stack(nb * p_inv), stack(k * p_inv)], axis=0).astype(BF16))
            rhs_end.append(jnp.concatenate([stack(nb * p_end), stack(k * p_end)], axis=0).astype(BF16))
            vsb.append(stack(v).astype(BF16))
            sb.append(s_ref[p].astype(BF16))
            decay.append(jnp.exp2(cl_last))
        gram = [_dot_nt(lhs[p], rhs[p]) for p in pairs]
        ws = [_dot_nt(lhs_flat[p], sb[p]) for p in pairs]
        a_ab = [jnp.where(strict, gram[p][:2 * c, :2 * c], 0.0) for p in pairs]
        a_ak = [jnp.where(strict, gram[p][:2 * c, 2 * c:], 0.0).astype(BF16) for p in pairs]
        a_r = [jnp.concatenate([jnp.where(incl, gram[p][2 * c:, :2 * c], 0.0),
                                jnp.where(incl, gram[p][2 * c:, 2 * c:], 0.0)], axis=1).astype(BF16)
               for p in pairs]
        u = [stack(ws[p][:c]) + _dot(a_ak[p], vsb[p]) for p in pairs]
        apow = a_ab
        steps = (c - 1).bit_length()
        for it in range(steps):
            ab = [apow[p].astype(BF16) for p in pairs]
            if it + 1 < steps:
                both = [_dot(ab[p], jnp.concatenate([u[p].astype(BF16), ab[p]], axis=1))
                        for p in pairs]
                u = [u[p] + both[p][:, :LANES] for p in pairs]
                apow = [both[p][:, LANES:] for p in pairs]
            else:
                u = [u[p] + _dot(ab[p], u[p].astype(BF16)) for p in pairs]
        uv = [jnp.concatenate([u[p].astype(BF16), vsb[p]], axis=0) for p in pairs]
        ys = [_dot(a_r[p], uv[p]) for p in pairs]
        for p in pairs:
            s_ref[p] = s_ref[p] * decay[p] + _dot_tn(uv[p], rhs_end[p])
        for p in pairs:
            bb, sl = where(p)
            y = ys[p][:c, :] + ys[p][c:, :] + ws[p][c:, :]
            mean = hsum(y) * (1.0 / hd)
            yc = y - mean
            var = hsum(yc * yc) * (1.0 / hd)
            yn = yc * lax.rsqrt(var + RWKV_LNX_EPS) * vec_ref[1:2, sl] + vec_ref[2:3, sl]
            r, k, v = (x[bb, rows, sl].astype(F32) for x in (r_ref, k_ref, v_ref))
            yn = yn + hsum(r * k * vec_ref[0:1, sl]) * v
            o_ref[bb, rows, sl] = (yn * gt_ref[bb, rows, sl].astype(F32)).astype(o_ref.dtype)
        return carry

    lax.fori_loop(0, n_chunks, chunk, 0, unroll=2)


def _rwkv_core(r, k, v, lw, na, nb, gt, vec, bsz, seq):
    t, d = r.shape
    ts = RWKV_SEQ_BLOCK
    nr = RWKV_BATCH_ROWS
    width = RWKV_PAIRS * LANES
    tok = pl.BlockSpec((nr, ts, width), lambda b, p, s: (b, s, p))
    seqs = [x.reshape(bsz, seq, d) for x in (r, k, v, lw, na, nb, gt)]
    out = pl.pallas_call(
        functools.partial(_rwkv_body, n_chunks=ts // RWKV_CHUNK, n_pairs=RWKV_PAIRS),
        grid=(bsz // nr, d // width, seq // ts),
        in_specs=[tok] * 7 + [pl.BlockSpec((vec.shape[0], width), lambda b, p, s: (0, p))],
        out_specs=tok,
        out_shape=jax.ShapeDtypeStruct((bsz, seq, d), BF16),
        scratch_shapes=[pltpu.VMEM((nr * RWKV_PAIRS, LANES, LANES), F32)],
        compiler_params=_params(("parallel", "parallel", "arbitrary")),
        name="rwkv_core",
    )(*seqs, vec)
    return out.reshape(t, d)


def _pad_to(x, axis, size):
    pads = [(0, 0)] * x.ndim
    pads[axis] = (0, size - x.shape[axis])
    return jnp.pad(x, pads)


def _rwkv_layer(h, g6, mu, w_rkv, w0, w1, w2, a0, a1, a2, g1, g2, k_k, k_a, r_k, lnx_g, lnx_b,
                w_o, bsz, seq):
    d = h.shape[1]
    up = lambda n: -(-n // LANES) * LANES
    w1p = _pad_to(w1, 1, up(w1.shape[1])).astype(BF16)
    w2p = _pad_to(w2, 0, up(w2.shape[0])).astype(BF16)
    a1p = _pad_to(a1, 1, up(a1.shape[1])).astype(BF16)
    a2p = _pad_to(a2, 0, up(a2.shape[0])).astype(BF16)
    g1p = _pad_to(g1, 1, up(g1.shape[1])).astype(BF16)
    g2p = _pad_to(g2, 0, up(g2.shape[0])).astype(BF16)
    vec_pre = _pad_to(jnp.stack([w0, a0, k_k, k_a]), 0, 8)
    outs = _rwkv_pre(h, g6, _pad_to(mu, 0, 8), w_rkv.astype(BF16), w1p, w2p, a1p, a2p, g1p, g2p,
                     vec_pre, seq)
    vec_core = _pad_to(jnp.stack([r_k.reshape(d), lnx_g, lnx_b]), 0, 8)
    o = _rwkv_core(*outs, vec_core, bsz, seq)
    return o, w_o.astype(BF16), jnp.zeros((1, d), F32)


def kernel(x, norm_g, ffn_w_in, ffn_w_out, swa_w_qkv, swa_b_qkv, swa_sinks, swa_w_o, swa_b_o, gla_w_in, gla_w_gate2, gla_b_gate, gla_norm_g, gla_w_o, rwkv_mu, rwkv_w_rkv, rwkv_w0, rwkv_w1, rwkv_w2, rwkv_a0, rwkv_a1, rwkv_a2, rwkv_g1, rwkv_g2, rwkv_k_k, rwkv_k_a, rwkv_r_k, rwkv_lnx_g, rwkv_lnx_b, rwkv_w_o):
    bsz, seq, d = x.shape
    depth = norm_g.shape[0]
    h = x.reshape(bsz * seq, d)
    win, wout = ffn_w_in[:, 1].astype(BF16), ffn_w_out
    for layer in range(depth):
        g6 = norm_g[layer]
        h = _ffn(h, g6, ffn_w_in, wout, layer, 0)
        kind, j = layer % 3, layer // 3
        if kind == 0:
            mix = _swa_layer(h, g6, swa_w_qkv[j], swa_b_qkv[j], swa_sinks[j], swa_w_o[j],
                             swa_b_o[j], bsz, seq)
        elif kind == 1:
            mix = _gla_layer(h, g6, gla_w_in[j], gla_w_gate2[j], gla_b_gate[j], gla_norm_g[j],
                             gla_w_o[j], bsz, seq)
        else:
            mix = _rwkv_layer(h, g6, rwkv_mu[j], rwkv_w_rkv[j], rwkv_w0[j], rwkv_w1[j], rwkv_w2[j],
                              rwkv_a0[j], rwkv_a1[j], rwkv_a2[j], rwkv_g1[j], rwkv_g2[j],
                              rwkv_k_k[j], rwkv_k_a[j], rwkv_r_k[j], rwkv_lnx_g[j], rwkv_lnx_b[j],
                              rwkv_w_o[j], bsz, seq)
        h = _ffn(h, g6, win, wout, layer, 1, mixer=mix)
    return h.reshape(bsz, seq, d)
```

```python
import functools

import jax
import jax.numpy as jnp
from jax import lax
from jax.experimental import pallas as pl
from jax.experimental.pallas import tpu as pltpu

F32 = jnp.float32
BF16 = jnp.bfloat16

NORM_EPS = 1e-6
GLA_NORM_EPS = 1e-5
GLA_GATE_TEMP = 16.0
RWKV_LNX_EPS = 64e-5
LOG2_E = 1.4426950408889634

D_FF_CHUNK = 256
TOKEN_BLOCK = 512
PROJ_TOKEN_BLOCK = 1024
PROJ_SUB_ROWS = 512
FFN_TOKEN_BLOCK = 1024
FFN_SUB_ROWS = 512
SWA_BLOCK = 128
SWA_Q_BLOCKS = 8
SWA_HEAD_DIM = 64
SWA_HEADS = 16
SWA_KV_HEADS = 4
GLA_HEADS = 4
GLA_CHUNK = 64
GLA_SEQ_BLOCK = 1024
RWKV_HEAD = 64
RWKV_CHUNK = 64
RWKV_SEQ_BLOCK = 512
RWKV_PAIRS = 8
RWKV_BATCH_ROWS = 2
LANES = 128
VMEM_LIMIT = 56 * 1024 * 1024


def _params(sem):
    return pltpu.CompilerParams(dimension_semantics=sem, vmem_limit_bytes=VMEM_LIMIT)


def _rmsnorm(x, g, eps=NORM_EPS):
    return x * lax.rsqrt(jnp.mean(x * x, axis=-1, keepdims=True) + eps) * g


def _sigmoid(x):
    return 1.0 / (1.0 + jnp.exp(-x))


def _softplus(x):
    return jnp.maximum(x, 0.0) + jnp.log(1.0 + jnp.exp(-jnp.abs(x)))


def _dot(a, b):
    return jnp.dot(a, b, preferred_element_type=F32)


def _dot_nt(a, b):
    return lax.dot_general(a, b, (((1,), (1,)), ((), ())), preferred_element_type=F32)


def _dot_tn(a, b):
    return lax.dot_general(a, b, (((0,), (0,)), ((), ())), preferred_element_type=F32)


def _full(shape):
    n = len(shape)
    return pl.BlockSpec(shape, lambda *_: (0,) * n)


def _ffn_body(*refs, g_row, mixer_proj):
    if mixer_proj:
        h_ref, a_ref, wo_ref, bo_ref, g_ref, win_ref, wout_ref, o_ref, xn_ref, acc_ref = refs
    else:
        h_ref, g_ref, win_ref, wout_ref, o_ref, xn_ref, acc_ref = refs
    d_ff = wout_ref.shape[0]
    tf = D_FF_CHUNK
    tm = h_ref.shape[0]
    subs = [slice(r0, r0 + FFN_SUB_ROWS) for r0 in range(0, tm, FFN_SUB_ROWS)]
    for rs in subs:
        hs = h_ref[rs, :]
        if mixer_proj:
            m = _dot(a_ref[rs, :], wo_ref[...]) + bo_ref[...]
            hs = hs + _rmsnorm(m, g_ref[g_row - 1:g_row, :])
            o_ref[rs, :] = hs
        xn_ref[rs, :] = _rmsnorm(hs, g_ref[g_row:g_row + 1, :]).astype(BF16)
    res_ref = o_ref if mixer_proj else h_ref
    for rs in subs:
        for c0 in range(0, d_ff, tf):
            xn = xn_ref[rs, :]
            gate = _dot(xn, win_ref[:, c0:c0 + tf].astype(BF16))
            up = _dot(xn, win_ref[:, d_ff + c0:d_ff + c0 + tf].astype(BF16))
            act = (gate * _sigmoid(gate) * up).astype(BF16)
            part = _dot(act, wout_ref[c0:c0 + tf, :].astype(BF16))
            if c0 == 0:
                acc_ref[rs, :] = part
            else:
                acc_ref[rs, :] += part
    for rs in subs:
        o_ref[rs, :] = res_ref[rs, :] + 0.5 * _rmsnorm(acc_ref[rs, :], g_ref[g_row + 1:g_row + 2, :])


def _resident(shape):
    n = len(shape)
    return pl.BlockSpec(shape, lambda *_: (0,) * n, pipeline_mode=pl.Buffered(1))


def _stacked_weight(w, *lead):
    idx = tuple(lead) + (0, 0)
    return pl.BlockSpec((None,) * len(lead) + w.shape[-2:], lambda *_: idx,
                        pipeline_mode=pl.Buffered(1))


def _ffn(h, g6, win, wout, layer, which, mixer=None):
    t, d = h.shape
    g_row = 4 * which
    tm = FFN_TOKEN_BLOCK
    tok = pl.BlockSpec((tm, d), lambda i: (i, 0))
    ins, in_specs = [h], [tok]
    if mixer is not None:
        a, w_o, b_o = mixer
        ins += [a, w_o, b_o]
        in_specs += [pl.BlockSpec((tm, a.shape[1]), lambda i: (i, 0)), _resident(w_o.shape),
                     _full(b_o.shape)]
    ins += [g6, win, wout]
    win_lead = (layer, which) if win.ndim == 4 else (layer,)
    in_specs += [_full(g6.shape), _stacked_weight(win, *win_lead),
                 _stacked_weight(wout, layer, which)]
    return pl.pallas_call(
        functools.partial(_ffn_body, g_row=g_row, mixer_proj=mixer is not None),
        grid=(t // tm,),
        in_specs=in_specs,
        out_specs=tok,
        out_shape=jax.ShapeDtypeStruct((t, d), F32),
        scratch_shapes=[pltpu.VMEM((tm, d), BF16), pltpu.VMEM((tm, d), F32)],
        compiler_params=_params(("parallel",)),
        name="ffn_mix" if mixer is not None else "ffn",
    )(*ins)


def _row_subs(tm):
    return [slice(r0, r0 + PROJ_SUB_ROWS) for r0 in range(0, tm, PROJ_SUB_ROWS)]


def _norm_proj_body(h_ref, g_ref, w_ref, b_ref, *o_refs, g_row, n_chunk):
    subs = _row_subs(h_ref.shape[0])
    xns = [_rmsnorm(h_ref[rs, :], g_ref[g_row:g_row + 1, :]).astype(BF16) for rs in subs]
    for xn, rs in zip(xns, subs):
        col = 0
        for o_ref in o_refs:
            n = o_ref.shape[1]
            for c in range(0, n, n_chunk):
                cs = slice(col + c, col + min(c + n_chunk, n))
                y = _dot(xn, w_ref[:, cs]) + b_ref[:, cs]
                o_ref[rs, c:c + cs.stop - cs.start] = y.astype(o_ref.dtype)
            col += n


def _norm_proj(h, g6, g_row, w, b, widths):
    t, d = h.shape
    tm = PROJ_TOKEN_BLOCK
    return pl.pallas_call(
        functools.partial(_norm_proj_body, g_row=g_row, n_chunk=512),
        grid=(t // tm,),
        in_specs=[
            pl.BlockSpec((tm, d), lambda i: (i, 0)),
            _full(g6.shape),
            _resident(w.shape),
            _full(b.shape),
        ],
        out_specs=[pl.BlockSpec((tm, n), lambda i: (i, 0)) for n in widths],
        out_shape=[jax.ShapeDtypeStruct((t, n), BF16) for n in widths],
        compiler_params=_params(("parallel",)),
        name="norm_proj",
    )(h, g6, w, b)


def _swa_body(sink_ref, q_ref, kp_ref, kc_ref, vp_ref, vc_ref, o_ref):
    first_step = pl.program_id(1) == 0
    blk = SWA_BLOCK
    nkv = SWA_KV_HEADS
    nq = q_ref.shape[0] // blk
    kall = jnp.concatenate([kp_ref[...], kc_ref[...]], axis=0)
    vall = jnp.concatenate([vp_ref[...], vc_ref[...]], axis=0)
    own = (lax.broadcasted_iota(jnp.int32, (blk, blk), 1)
           <= lax.broadcasted_iota(jnp.int32, (blk, blk), 0))
    no_prev = jnp.where(first_step, -jnp.inf, 0.0)
    width = nkv * SWA_HEAD_DIM
    lane_head = lax.broadcasted_iota(jnp.int32, kall.shape, 1) // SWA_HEAD_DIM
    scale = SWA_HEAD_DIM ** -0.5
    groups = SWA_HEADS // nkv
    kmask = [kall * jnp.where(lane_head == hh, scale, 0.0).astype(BF16) for hh in range(nkv)]
    vmask = [vall * jnp.where(lane_head == hh, 1.0, 0.0).astype(BF16) for hh in range(nkv)]
    work = [(j, s) for j in range(nq) for s in range(groups)]

    def keys_of(masked, j):
        return jnp.concatenate([m[j * blk:(j + 2) * blk] for m in masked], axis=0)

    def softmax(j, s, sc4):
        ps = []
        for hh in range(nkv):
            c0 = hh * 2 * blk
            prev = sc4[:, c0:c0 + blk]
            if j == 0:
                prev = prev + no_prev
            sc = jnp.where(own, sc4[:, c0 + blk:c0 + 2 * blk], prev)
            sink = sink_ref[hh * groups + s]
            m = jnp.maximum(jnp.max(sc, axis=-1, keepdims=True), sink)
            p = jnp.exp(sc - m)
            inv = 1.0 / (jnp.sum(p, axis=-1, keepdims=True) + jnp.exp(sink - m))
            p = p * inv
            ps.append(jnp.where(own, 0.0, p).astype(BF16))
            ps.append(jnp.where(own, p, 0.0).astype(BF16))
        return jnp.concatenate(ps, axis=1)

    kstack = [keys_of(kmask, j) for j in range(nq)]
    vstack = [keys_of(vmask, j) for j in range(nq)]
    scores = [_dot_nt(q_ref[j * blk:(j + 1) * blk, s * width:(s + 1) * width], kstack[j])
              for j, s in work]
    probs = [softmax(j, s, scores[i]) for i, (j, s) in enumerate(work)]
    for i, (j, s) in enumerate(work):
        o_ref[j * blk:(j + 1) * blk, s * width:(s + 1) * width] = (
            _dot(probs[i], vstack[j]).astype(o_ref.dtype))


def _swa_core(q, k, v, sinks, bsz, seq):
    blk = SWA_BLOCK
    nq = SWA_Q_BLOCKS
    ns = seq // (nq * blk)
    dq, dkv = q.shape[1], k.shape[1]
    cur = lambda b, n: (b * ns + n, 0)
    prev = lambda b, n: ((b * ns + n) * nq - jnp.minimum(n, 1), 0)
    return pl.pallas_call(
        _swa_body,
        grid=(bsz, ns),
        in_specs=[
            pl.BlockSpec(memory_space=pltpu.SMEM),
            pl.BlockSpec((nq * blk, dq), cur),
            pl.BlockSpec((blk, dkv), prev),
            pl.BlockSpec((nq * blk, dkv), cur),
            pl.BlockSpec((blk, dkv), prev),
            pl.BlockSpec((nq * blk, dkv), cur),
        ],
        out_specs=pl.BlockSpec((nq * blk, dq), cur),
        out_shape=jax.ShapeDtypeStruct((bsz * seq, dq), BF16),
        compiler_params=_params(("parallel", "parallel")),
        name="swa_core",
    )(sinks, q, k, k, v, v)


def _swa_layer(h, g6, w_qkv, b_qkv, sinks, w_o, b_o, bsz, seq):
    d = h.shape[1]
    hd, nh, nkv = SWA_HEAD_DIM, SWA_HEADS, SWA_KV_HEADS
    grp = nh // nkv
    dq = nh * hd
    wq = w_qkv[:, :dq].reshape(d, nkv, grp, hd).transpose(0, 2, 1, 3).reshape(d, dq)
    bq = b_qkv[:dq].reshape(nkv, grp, hd).transpose(1, 0, 2).reshape(dq)
    w = jnp.concatenate([wq, w_qkv[:, dq:]], axis=1).astype(BF16)
    b = jnp.concatenate([bq, b_qkv[dq:]])[None, :]
    wo = w_o.reshape(nkv, grp, hd, d).transpose(1, 0, 2, 3).reshape(dq, d).astype(BF16)
    dkv = nkv * hd
    q, k, v = _norm_proj(h, g6, 2, w, b, (dq, dkv, dkv))
    att = _swa_core(q, k, v, sinks, bsz, seq)
    return att, wo, b_o[None, :]


def _split3(x):
    hi = x.astype(BF16)
    r1 = x - hi.astype(F32)
    mid = r1.astype(BF16)
    lo = (r1 - mid.astype(F32)).astype(BF16)
    return hi, mid, lo


def _gla_in_body(h_ref, g_ref, w_ref, wl_ref, w2_ref, bg_ref, y_ref, b_ref, *, n_chunk):
    c = GLA_CHUNK
    subs = _row_subs(h_ref.shape[0])
    tri = jnp.where(lax.broadcasted_iota(jnp.int32, (c, c), 0)
                    >= lax.broadcasted_iota(jnp.int32, (c, c), 1), 1.0, 0.0).astype(BF16)
    xns = [_rmsnorm(h_ref[rs, :], g_ref[2:3, :]).astype(BF16) for rs in subs]
    for xn, rs in zip(xns, subs):
        g_low = _dot(xn, wl_ref[...]).astype(BF16)
        z = _dot(g_low, w2_ref[...]) + bg_ref[...]
        la = -_softplus(-z) * (LOG2_E / GLA_GATE_TEMP)
        parts = [_split3(la[r0:r0 + c, :]) for r0 in range(0, PROJ_SUB_ROWS, c)]
        for c0 in range(0, w_ref.shape[1], n_chunk):
            y_ref[rs, c0:c0 + n_chunk] = _dot(xn, w_ref[:, c0:c0 + n_chunk]).astype(y_ref.dtype)
        for i, (hi, mid, lo) in enumerate(parts):
            r0 = rs.start + i * c
            b_ref[r0:r0 + c, :] = _dot(tri, hi) + _dot(tri, mid) + _dot(tri, lo)


def _gla_in(h, g6, w_main, w_low, w_gate2, b_gate):
    t, d = h.shape
    n = w_main.shape[1]
    dk = w_gate2.shape[1]
    tm = PROJ_TOKEN_BLOCK
    return pl.pallas_call(
        functools.partial(_gla_in_body, n_chunk=512),
        grid=(t // tm,),
        in_specs=[
            pl.BlockSpec((tm, d), lambda i: (i, 0)),
            _full(g6.shape),
            _resident(w_main.shape),
            _resident(w_low.shape),
            _resident(w_gate2.shape),
            _full(b_gate.shape),
        ],
        out_specs=[pl.BlockSpec((tm, n), lambda i: (i, 0)), pl.BlockSpec((tm, dk), lambda i: (i, 0))],
        out_shape=[jax.ShapeDtypeStruct((t, n), BF16), jax.ShapeDtypeStruct((t, dk), F32)],
        compiler_params=_params(("parallel",)),
        name="gla_in",
    )(h, g6, w_main, w_low, w_gate2, b_gate)


def _block_first(x, s):
    c, l = x.shape
    xb = x.reshape(c // s, s, l)
    return jnp.broadcast_to(xb[:, 0:1, :], (c // s, s, l)).reshape(c, l)


def _gla_body(q_ref, k_ref, v_ref, go_ref, b_ref, ng_ref, o_ref, st_ref, *, n_chunks, n_heads):
    c = GLA_CHUNK
    dk = q_ref.shape[1] // n_heads
    dv = v_ref.shape[1] // n_heads
    heads = range(n_heads)

    @pl.when(pl.program_id(1) == 0)
    def _():
        st_ref[...] = jnp.zeros_like(st_ref)

    row = lax.broadcasted_iota(jnp.int32, (c, 1), 0)
    ri = lax.broadcasted_iota(jnp.int32, (c, c), 0)
    ci = lax.broadcasted_iota(jnp.int32, (c, c), 1)
    levels = (8, 16, 32)
    odd = [(row // s) % 2 == 1 for s in levels]
    same_group = [ri // (2 * s) == ci // (2 * s) for s in levels]
    hit = [(ci == 8 * (ri // 8) + jj) & (ri % 8 >= jj) for jj in range(8)]

    def chunk(ic, carry):
        rows = pl.ds(pl.multiple_of(ic * c, c), c)
        q, k, v, b = [], [], [], []
        for h in heads:
            q.append(q_ref[rows, h * dk:(h + 1) * dk].astype(F32) * (dk ** -0.5))
            k.append(k_ref[rows, h * dk:(h + 1) * dk].astype(F32))
            v.append(v_ref[rows, h * dv:(h + 1) * dv])
            b.append(b_ref[rows, h * dk:(h + 1) * dk])
        st = [st_ref[h] for h in heads]
        o = [_dot_nt((q[h] * jnp.exp2(b[h])).astype(BF16), st[h].astype(BF16)) for h in heads]
        for h in heads:
            b_last = b[h][c - 1:c, :]
            kd = (k[h] * jnp.exp2(b_last - b[h])).astype(BF16)
            st_ref[h] = st[h] * jnp.exp2(b_last) + _dot_tn(v[h], kd)
        al = []
        for h in heads:
            for i, s in enumerate(levels):
                first = _block_first(b[h], s)
                nxt = jnp.concatenate([first[s:], first[:s]], axis=0)
                qh = jnp.where(odd[i], q[h] * jnp.exp2(b[h] - first), 0.0)
                kh = jnp.where(odd[i], 0.0, k[h] * jnp.exp2(jnp.minimum(nxt - b[h], 0.0)))
                al.append(_dot_nt(qh.astype(BF16), kh.astype(BF16)))
        a = []
        for h in heads:
            ah = jnp.zeros((c, c), F32)
            for i in range(len(levels)):
                ah = ah + jnp.where(same_group[i], al[h * len(levels) + i], 0.0)
            k3 = k[h].reshape(c // 8, 8, dk)
            b3 = b[h].reshape(c // 8, 8, dk)
            for jj in range(8):
                kj = jnp.broadcast_to(k3[:, jj:jj + 1, :], k3.shape).reshape(c, dk)
                bj = jnp.broadcast_to(b3[:, jj:jj + 1, :], b3.shape).reshape(c, dk)
                tt = q[h] * kj * jnp.exp2(jnp.minimum(b[h] - bj, 0.0))
                ah = jnp.where(hit[jj], jnp.sum(tt, axis=-1, keepdims=True), ah)
            a.append(ah.astype(BF16))
        o = [o[h] + _dot(a[h], v[h]) for h in heads]
        for h in heads:
            on = _rmsnorm(o[h], ng_ref[...], GLA_NORM_EPS)
            go = go_ref[rows, h * dv:(h + 1) * dv].astype(F32)
            o_ref[rows, h * dv:(h + 1) * dv] = (on * (go * _sigmoid(go))).astype(o_ref.dtype)
        return carry

    lax.fori_loop(0, n_chunks, chunk, 0, unroll=2)


def _gla_core(y, bcum, norm_g, bsz, seq):
    nh = GLA_HEADS
    dqk = bcum.shape[1]
    dvv = (y.shape[1] - 2 * dqk) // 2
    ts = GLA_SEQ_BLOCK
    ns = seq // ts
    rowblk = lambda b, s: b * ns + s
    return pl.pallas_call(
        functools.partial(_gla_body, n_chunks=ts // GLA_CHUNK, n_heads=nh),
        grid=(bsz, ns),
        in_specs=[
            pl.BlockSpec((ts, dqk), lambda b, s: (rowblk(b, s), 0)),
            pl.BlockSpec((ts, dqk), lambda b, s: (rowblk(b, s), 1)),
            pl.BlockSpec((ts, dvv), lambda b, s: (rowblk(b, s), (2 * dqk) // dvv)),
            pl.BlockSpec((ts, dvv), lambda b, s: (rowblk(b, s), (2 * dqk) // dvv + 1)),
            pl.BlockSpec((ts, dqk), lambda b, s: (rowblk(b, s), 0)),
            _full(norm_g.shape),
        ],
        out_specs=pl.BlockSpec((ts, dvv), lambda b, s: (rowblk(b, s), 0)),
        out_shape=jax.ShapeDtypeStruct((bsz * seq, dvv), BF16),
        scratch_shapes=[pltpu.VMEM((nh, dvv // nh, dqk // nh), F32)],
        compiler_params=_params(("parallel", "arbitrary")),
        name="gla_core",
    )(y, y, y, y, bcum, norm_g)


def _gla_layer(h, g6, w_in, w_gate2, b_gate, norm_g, w_o, bsz, seq):
    d = h.shape[1]
    rank = w_gate2.shape[0]
    n_main = w_in.shape[1] - rank
    w_main = w_in[:, :n_main].astype(BF16)
    w_low = jnp.pad(w_in[:, n_main:], ((0, 0), (0, LANES - rank))).astype(BF16)
    w2 = jnp.pad(w_gate2, ((0, LANES - rank), (0, 0))).astype(BF16)
    y, bcum = _gla_in(h, g6, w_main, w_low, w2, b_gate[None, :])
    o = _gla_core(y, bcum, norm_g[None, :], bsz, seq)
    return o, w_o.astype(BF16), jnp.zeros((1, d), F32)


def _rwkv_pre_body(h_ref, hp_ref, g_ref, mu_ref, wrkv_ref, w1_ref, w2_ref, a1_ref, a2_ref,
                   g1_ref, g2_ref, vec_ref, r_ref, k_ref, v_ref, lw_ref, na_ref, nb_ref, gt_ref,
                   *, blocks_per_seq):
    i = pl.program_id(0)
    tm, d = h_ref.shape
    gain = g_ref[2:3, :]
    x = _rmsnorm(h_ref[...], gain)
    prev = _rmsnorm(hp_ref[...], gain)[7:8, :]
    prev = jnp.where(i % blocks_per_seq == 0, 0.0, prev)
    row = lax.broadcasted_iota(jnp.int32, (tm, 1), 0)
    xs = jnp.where(row == 0, prev, pltpu.roll(x, shift=1, axis=0))
    xx = xs - x
    li = lax.broadcasted_iota(jnp.int32, (LANES, LANES), 0) // RWKV_HEAD
    lj = lax.broadcasted_iota(jnp.int32, (LANES, LANES), 1) // RWKV_HEAD
    gmat = jnp.where(li == lj, 1.0, 0.0).astype(BF16)

    def mix(j):
        return (x + xx * mu_ref[j:j + 1, :]).astype(BF16)

    t_a = _dot(mix(4), a1_ref[...]).astype(BF16)
    t_w = jnp.tanh(_dot(mix(1), w1_ref[...])).astype(BF16)
    t_g = _sigmoid(_dot(mix(5), g1_ref[...])).astype(BF16)
    k = _dot(mix(2), wrkv_ref[1])
    a = _sigmoid(vec_ref[1:2, :] + _dot(t_a, a2_ref[...]))
    kk = k * vec_ref[2:3, :]
    sq = [(kk[:, s:s + LANES] * kk[:, s:s + LANES]).astype(BF16) for s in range(0, d, LANES)]
    v_ref[...] = _dot(mix(3), wrkv_ref[2]).astype(v_ref.dtype)
    k_ref[...] = (k * (1.0 + (a - 1.0) * vec_ref[3:4, :])).astype(k_ref.dtype)
    for i, s in enumerate(range(0, d, LANES)):
        ss = _dot(sq[i], gmat)
        kkn = kk[:, s:s + LANES] / jnp.maximum(jnp.sqrt(ss), 1e-12)
        na_ref[:, s:s + LANES] = (-kkn).astype(na_ref.dtype)
        nb_ref[:, s:s + LANES] = (kkn * a[:, s:s + LANES]).astype(nb_ref.dtype)
    r_ref[...] = _dot(mix(0), wrkv_ref[0]).astype(r_ref.dtype)
    zw = vec_ref[0:1, :] + _dot(t_w, w2_ref[...])
    gt_ref[...] = _dot(t_g, g2_ref[...]).astype(gt_ref.dtype)
    lw_ref[...] = -LOG2_E * jnp.exp(-_softplus(-zw) - 0.5)


def _rwkv_pre(h, g6, mu, wrkv, w1, w2, a1, a2, g1, g2, vec, seq):
    t, d = h.shape
    tm = TOKEN_BLOCK
    tok = pl.BlockSpec((tm, d), lambda i: (i, 0))
    ins = [h, h, g6, mu, wrkv, w1, w2, a1, a2, g1, g2, vec]
    in_specs = [tok, pl.BlockSpec((8, d), lambda i: (jnp.maximum(i * (tm // 8) - 1, 0), 0))]
    in_specs += [_full(x.shape) for x in ins[2:]]
    return pl.pallas_call(
        functools.partial(_rwkv_pre_body, blocks_per_seq=seq // tm),
        grid=(t // tm,),
        in_specs=in_specs,
        out_specs=[tok] * 7,
        out_shape=[jax.ShapeDtypeStruct((t, d), F32 if i == 3 else BF16) for i in range(7)],
        compiler_params=_params(("parallel",)),
        name="rwkv_pre",
    )(*ins)


def _rwkv_body(r_ref, k_ref, v_ref, lw_ref, na_ref, nb_ref, gt_ref, vec_ref, o_ref, s_ref,
               *, n_chunks, n_pairs):
    c = RWKV_CHUNK
    hd = RWKV_HEAD
    n_rows = r_ref.shape[0]
    pairs = range(n_rows * n_pairs)

    def where(p):
        return p // n_pairs, slice((p % n_pairs) * LANES, (p % n_pairs + 1) * LANES)

    @pl.when(pl.program_id(2) == 0)
    def _():
        s_ref[...] = jnp.zeros_like(s_ref)

    lane = lax.broadcasted_iota(jnp.int32, (c, LANES), 1)
    head0 = lane < hd
    ri = lax.broadcasted_iota(jnp.int32, (2 * c, 2 * c), 0)
    ci = lax.broadcasted_iota(jnp.int32, (2 * c, 2 * c), 1)
    strict = ri > ci
    incl = ri >= ci
    tri = jnp.where(lax.broadcasted_iota(jnp.int32, (c, c), 0)
                    >= lax.broadcasted_iota(jnp.int32, (c, c), 1), 1.0, 0.0).astype(BF16)

    def stack(x):
        return jnp.concatenate([jnp.where(head0, x, 0.0), jnp.where(head0, 0.0, x)], axis=0)

    def hsum(x):
        s0 = jnp.sum(jnp.where(head0, x, 0.0), axis=-1, keepdims=True)
        s1 = jnp.sum(jnp.where(head0, 0.0, x), axis=-1, keepdims=True)
        return jnp.where(head0, s0, s1)

    def chunk(ic, carry):
        r0 = pl.multiple_of(ic * c, c)
        rows = pl.ds(r0, c)
        cl_all = []
        for bb in range(n_rows):
            hi, mid, lo = _split3(lw_ref[bb, rows, :])
            cl_all.append(_dot(tri, hi) + _dot(tri, mid) + _dot(tri, lo))
        lhs, lhs_flat, rhs, rhs_end, vsb, sb, decay = [], [], [], [], [], [], []
        for p in pairs:
            bb, sl = where(p)
            r, k, v = (x[bb, rows, sl].astype(F32) for x in (r_ref, k_ref, v_ref))
            lw = lw_ref[bb, rows, sl]
            na, nb = na_ref[bb, rows, sl].astype(F32), nb_ref[bb, rows, sl].astype(F32)
            cl = cl_all[bb][:, sl]
            cl_last = cl[c - 1:c, :]
            p_inv = jnp.exp2(-cl)
            p_end = jnp.exp2(cl_last - cl)
            at, rt = na * jnp.exp2(cl - lw), r * jnp.exp2(cl)
            lhs.append(jnp.concatenate([stack(at), stack(rt)], axis=0).astype(BF16))
            lhs_flat.append(jnp.concatenate([at, rt], axis=0).astype(BF16))
            rhs.append(jnp.concatenate([stack(nb * p_inv), stack(k * p_inv)], axis=0).astype(BF16))
            rhs_end.append(jnp.concatenate([stack(nb * p_end), stack(k * p_end)], axis=0).astype(BF16))
            vsb.append(stack(v).astype(BF16))
            sb.append(s_ref[p].astype(BF16))
            decay.append(jnp.exp2(cl_last))
        gram = [_dot_nt(lhs[p], rhs[p]) for p in pairs]
        ws = [_dot_nt(lhs_flat[p], sb[p]) for p in pairs]
        a_ab = [jnp.where(strict, gram[p][:2 * c, :2 * c], 0.0) for p in pairs]
        a_ak = [jnp.where(strict, gram[p][:2 * c, 2 * c:], 0.0).astype(BF16) for p in pairs]
        a_r = [jnp.concatenate([jnp.where(incl, gram[p][2 * c:, :2 * c], 0.0),
                                jnp.where(incl, gram[p][2 * c:, 2 * c:], 0.0)], axis=1).astype(BF16)
               for p in pairs]
        u = [stack(ws[p][:c]) + _dot(a_ak[p], vsb[p]) for p in pairs]
        apow = a_ab
        steps = (c - 1).bit_length()
        for it in range(steps):
            ab = [apow[p].astype(BF16) for p in pairs]
            if it + 1 < steps:
                both = [_dot(ab[p], jnp.concatenate([u[p].astype(BF16), ab[p]], axis=1))
                        for p in pairs]
                u = [u[p] + both[p][:, :LANES] for p in pairs]
                apow = [both[p][:, LANES:] for p in pairs]
            else:
                u = [u[p] + _dot(ab[p], u[p].astype(BF16)) for p in pairs]
        uv = [jnp.concatenate([u[p].astype(BF16), vsb[p]], axis=0) for p in pairs]
        ys = [_dot(a_r[p], uv[p]) for p in pairs]
        for p in pairs:
            s_ref[p] = s_ref[p] * decay[p] + _dot_tn(uv[p], rhs_end[p])
        for p in pairs:
            bb, sl = where(p)
            y = ys[p][:c, :] + ys[p][c:, :] + ws[p][c:, :]
            mean = hsum(y) * (1.0 / hd)
            yc = y - mean
            var = hsum(yc * yc) * (1.0 / hd)
            yn = yc * lax.rsqrt(var + RWKV_LNX_EPS) * vec_ref[1:2, sl] + vec_ref[2:3, sl]
            r, k, v = (x[bb, rows, sl].astype(F32) for x in (r_ref, k_ref, v_ref))
            yn = yn + hsum(r * k * vec_ref[0:1, sl]) * v
            o_ref[bb, rows, sl] = (yn * gt_ref[bb, rows, sl].astype(F32)).astype(o_ref.dtype)
        return carry

    lax.fori_loop(0, n_chunks, chunk, 0, unroll=2)


def _rwkv_core(r, k, v, lw, na, nb, gt, vec, bsz, seq):
    t, d = r.shape
    ts = RWKV_SEQ_BLOCK
    nr = RWKV_BATCH_ROWS
    width = RWKV_PAIRS * LANES
    tok = pl.BlockSpec((nr, ts, width), lambda b, p, s: (b, s, p))
    seqs = [x.reshape(bsz, seq, d) for x in (r, k, v, lw, na, nb, gt)]
    out = pl.pallas_call(
        functools.partial(_rwkv_body, n_chunks=ts // RWKV_CHUNK, n_pairs=RWKV_PAIRS),
        grid=(bsz // nr, d // width, seq // ts),
        in_specs=[tok] * 7 + [pl.BlockSpec((vec.shape[0], width), lambda b, p, s: (0, p))],
        out_specs=tok,
        out_shape=jax.ShapeDtypeStruct((bsz, seq, d), BF16),
        scratch_shapes=[pltpu.VMEM((nr * RWKV_PAIRS, LANES, LANES), F32)],
        compiler_params=_params(("parallel", "parallel", "arbitrary")),
        name="rwkv_core",
    )(*seqs, vec)
    return out.reshape(t, d)


def _pad_to(x, axis, size):
    pads = [(0, 0)] * x.ndim
    pads[axis] = (0, size - x.shape[axis])
    return jnp.pad(x, pads)


def _rwkv_layer(h, g6, mu, w_rkv, w0, w1, w2, a0, a1, a2, g1, g2, k_k, k_a, r_k, lnx_g, lnx_b,
                w_o, bsz, seq):
    d = h.shape[1]
    up = lambda n: -(-n // LANES) * LANES
    w1p = _pad_to(w1, 1, up(w1.shape[1])).astype(BF16)
    w2p = _pad_to(w2, 0, up(w2.shape[0])).astype(BF16)
    a1p = _pad_to(a1, 1, up(a1.shape[1])).astype(BF16)
    a2p = _pad_to(a2, 0, up(a2.shape[0])).astype(BF16)
    g1p = _pad_to(g1, 1, up(g1.shape[1])).astype(BF16)
    g2p = _pad_to(g2, 0, up(g2.shape[0])).astype(BF16)
    vec_pre = _pad_to(jnp.stack([w0, a0, k_k, k_a]), 0, 8)
    outs = _rwkv_pre(h, g6, _pad_to(mu, 0, 8), w_rkv.astype(BF16), w1p, w2p, a1p, a2p, g1p, g2p,
                     vec_pre, seq)
    vec_core = _pad_to(jnp.stack([r_k.reshape(d), lnx_g, lnx_b]), 0, 8)
    o = _rwkv_core(*outs, vec_core, bsz, seq)
    return o, w_o.astype(BF16), jnp.zeros((1, d), F32)


def kernel(x, norm_g, ffn_w_in, ffn_w_out, swa_w_qkv, swa_b_qkv, swa_sinks, swa_w_o, swa_b_o, gla_w_in, gla_w_gate2, gla_b_gate, gla_norm_g, gla_w_o, rwkv_mu, rwkv_w_rkv, rwkv_w0, rwkv_w1, rwkv_w2, rwkv_a0, rwkv_a1, rwkv_a2, rwkv_g1, rwkv_g2, rwkv_k_k, rwkv_k_a, rwkv_r_k, rwkv_lnx_g, rwkv_lnx_b, rwkv_w_o):
    bsz, seq, d = x.shape
    depth = norm_g.shape[0]
    h = x.reshape(bsz * seq, d)
    win, wout = ffn_w_in[:, 1].astype(BF16), ffn_w_out
    for layer in range(depth):
        g6 = norm_g[layer]
        h = _ffn(h, g6, ffn_w_in, wout, layer, 0)
        kind, j = layer % 3, layer // 3
        if kind == 0:
            mix = _swa_layer(h, g6, swa_w_qkv[j], swa_b_qkv[j], swa_sinks[j], swa_w_o[j],
                             swa_b_o[j], bsz, seq)
        elif kind == 1:
            mix = _gla_layer(h, g6, gla_w_in[j], gla_w_gate2[j], gla_b_gate[j], gla_norm_g[j],
                             gla_w_o[j], bsz, seq)
        else:
            mix = _rwkv_layer(h, g6, rwkv_mu[j], rwkv_w_rkv[j], rwkv_w0[j], rwkv_w1[j], rwkv_w2[j],
                              rwkv_a0[j], rwkv_a1[j], rwkv_a2[j], rwkv_g1[j], rwkv_g2[j],
                              rwkv_k_k[j], rwkv_k_a[j], rwkv_r_k[j], rwkv_lnx_g[j], rwkv_lnx_b[j],
                              rwkv_w_o[j], bsz, seq)
        h = _ffn(h, g6, win, wout, layer, 1, mixer=mix)
    return h.reshape(bsz, seq, d)
```
